```python
import jax, jax.numpy as jnp
from jax import lax
import numpy as np

D_MODEL = 1024
BATCH = 32
SEQ = 2048
DEPTH = 1

SB_HEADS = 8
SB_HEAD_DIM = 128
SB_WIDTH = SB_HEADS * SB_HEAD_DIM
SB_BLOCK = 128
GDN_HEADS = 8
GDN_HEAD_DIM = 128
GDN_WIDTH = GDN_HEADS * GDN_HEAD_DIM
GDN_CONV = 4
GDN_CHUNK = 64
PEER_HEADS = 8
PEER_N_KEYS = 128
PEER_N_EXPERTS = PEER_N_KEYS * PEER_N_KEYS
PEER_D_KEY = 256
PEER_HALF = PEER_D_KEY // 2
PEER_TOPK = 16
PEER_TOKEN_BLOCK = 128
IN_WIDTH = 3 * SB_WIDTH + 3 * GDN_WIDTH + GDN_WIDTH + 2 * GDN_HEADS + 2 * D_MODEL
EPS = 1e-6

kernel_name = "hybrid_stickbreak_gdn_peer_block"


def rms_norm(x, w):
    xf = x.astype(jnp.float32)
    y = xf * lax.rsqrt(jnp.mean(xf * xf, axis=-1, keepdims=True) + EPS)
    return (y * w.astype(jnp.float32)).astype(x.dtype)


def l2_normalize(x):
    xf = x.astype(jnp.float32)
    return xf * lax.rsqrt(jnp.sum(xf * xf, axis=-1, keepdims=True) + EPS)


def stick_breaking_attention(q, k, v):
    S, Dh = q.shape[2], q.shape[3]
    scale = Dh ** -0.5
    qf, kf, vf = q.astype(jnp.float32), k.astype(jnp.float32), v.astype(jnp.float32)
    outs = []
    for blk in range(S // SB_BLOCK):
        t0 = blk * SB_BLOCK
        t1 = t0 + SB_BLOCK
        z = jnp.einsum('bhqd,bhkd->bhqk', qf[:, :, t0:t1], kf[:, :, :t1]) * scale
        t_idx = t0 + jnp.arange(SB_BLOCK)[:, None]
        s_idx = jnp.arange(t1)[None, :]
        causal = s_idx < t_idx
        log_1mb = jnp.where(causal, jax.nn.log_sigmoid(-z), 0.0)
        suffix = lax.cumsum(log_1mb, axis=3, reverse=True) - log_1mb
        log_a = jax.nn.log_sigmoid(z) + suffix
        a = jnp.where(causal, jnp.exp(log_a), 0.0)
        outs.append(jnp.einsum('bhqk,bhkd->bhqd', a, vf[:, :, :t1]))
    return jnp.concatenate(outs, axis=2).astype(q.dtype)


def causal_depthwise_conv(x, w):
    K = w.shape[0]
    S = x.shape[1]
    xp = jnp.pad(x, ((0, 0), (K - 1, 0), (0, 0)))
    return sum(xp[:, i:i + S] * w[i] for i in range(K))


def gated_delta_rule_chunked(q, k, v, g, beta):
    B, H, S, Dk = q.shape
    Dv = v.shape[-1]
    C = GDN_CHUNK
    n = S // C
    q = q * (Dk ** -0.5)
    qc = q.reshape(B, H, n, C, Dk)
    kc = k.reshape(B, H, n, C, Dk)
    vc = v.reshape(B, H, n, C, Dv)
    gc = jnp.cumsum(g.reshape(B, H, n, C), axis=-1)
    bc = beta.reshape(B, H, n, C)
    k_beta = kc * bc[..., None]
    v_beta = vc * bc[..., None]
    tril = jnp.tril(jnp.ones((C, C), dtype=bool))
    strict = jnp.tril(jnp.ones((C, C), dtype=bool), -1)
    gdiff = gc[..., :, None] - gc[..., None, :]
    decay = jnp.where(tril, jnp.exp(jnp.where(tril, gdiff, 0.0)), 0.0)
    L = jnp.where(strict, jnp.einsum('bhnid,bhnjd->bhnij', k_beta, kc) * decay, 0.0)
    eye = jnp.eye(C, dtype=L.dtype)
    T = lax.linalg.triangular_solve(eye + L, jnp.broadcast_to(eye, L.shape),
                                    left_side=True, lower=True, unit_diagonal=True)
    v_corr = jnp.einsum('bhnij,bhnjd->bhnid', T, v_beta)
    k_cumdecay = jnp.einsum('bhnij,bhnjd->bhnid', T, k_beta * jnp.exp(gc)[..., None])
    attn_intra = jnp.where(tril, jnp.einsum('bhnid,bhnjd->bhnij', qc, kc) * decay, 0.0)

    def step(state, inp):
        q_i, k_i, v_i, kcd_i, g_i, a_i = inp
        v_new = v_i - jnp.einsum('bhcd,bhde->bhce', kcd_i, state)
        o = (jnp.einsum('bhcd,bhde->bhce', q_i * jnp.exp(g_i)[..., None], state)
             + jnp.einsum('bhij,bhje->bhie', a_i, v_new))
        g_last = g_i[..., -1]
        k_w = k_i * jnp.exp(g_last[..., None] - g_i)[..., None]
        state = state * jnp.exp(g_last)[..., None, None] + jnp.einsum('bhcd,bhce->bhde', k_w, v_new)
        return state, o

    to_front = lambda t: jnp.moveaxis(t, 2, 0)
    xs = (to_front(qc), to_front(kc), to_front(v_corr), to_front(k_cumdecay), to_front(gc), to_front(attn_intra))
    state0 = jnp.zeros((B, H, Dk, Dv), dtype=jnp.float32)
    _, o = lax.scan(step, state0, xs)
    return jnp.moveaxis(o, 0, 2).reshape(B, H, S, Dv)


def peer_ffn(h, w_q, keys1, keys2, u_tab, v_tab):
    B, S, D = h.shape
    q = jnp.einsum('bsd,de->bse', h, w_q).astype(jnp.float32).reshape(B, S, PEER_HEADS, PEER_D_KEY)
    s1 = jnp.einsum('bshd,hnd->bshn', q[..., :PEER_HALF], keys1.astype(jnp.float32))
    s2 = jnp.einsum('bshd,hnd->bshn', q[..., PEER_HALF:], keys2.astype(jnp.float32))
    v1, i1 = lax.top_k(s1, PEER_TOPK)
    v2, i2 = lax.top_k(s2, PEER_TOPK)
    cand = (v1[..., :, None] + v2[..., None, :]).reshape(B, S, PEER_HEADS, PEER_TOPK * PEER_TOPK)
    vals, c = lax.top_k(cand, PEER_TOPK)
    e1 = jnp.take_along_axis(i1, c // PEER_TOPK, axis=-1)
    e2 = jnp.take_along_axis(i2, c % PEER_TOPK, axis=-1)
    expert = e1 * PEER_N_KEYS + e2
    gate = jax.nn.softmax(vals, axis=-1)
    n_tok = B * S
    nb = n_tok // PEER_TOKEN_BLOCK
    hk = PEER_HEADS * PEER_TOPK
    hx = h.reshape(nb, PEER_TOKEN_BLOCK, D)
    ex = expert.reshape(nb, PEER_TOKEN_BLOCK, hk)
    gx = gate.reshape(nb, PEER_TOKEN_BLOCK, hk).astype(h.dtype)

    def block(args):
        hb, eb, gb = args
        u = u_tab[eb]
        act = jax.nn.gelu(jnp.einsum('td,ted->te', hb, u), approximate=False)
        v = v_tab[eb]
        return jnp.einsum('te,ted->td', gb * act, v)

    out = lax.map(block, (hx, ex, gx))
    return out.reshape(B, S, D).astype(h.dtype)


def setup_inputs(seed: int = 0) -> dict:
    key = jax.random.key(seed)
    ks = jax.random.split(key, 20)
    f32 = jnp.float32
    nrm = lambda k, shape, scale: jax.random.normal(k, shape, f32) * scale
    gain = lambda k, shape: 1.0 + 0.02 * jax.random.normal(k, shape, f32)
    dt = jnp.exp(jax.random.uniform(ks[7], (DEPTH, GDN_HEADS), f32, np.log(1e-3), np.log(1e-1)))
    return {
        "x": jax.random.normal(ks[0], (BATCH, SEQ, D_MODEL), f32),
        "mix_norm_w": gain(ks[1], (DEPTH, D_MODEL)),
        "w_in": nrm(ks[2], (DEPTH, D_MODEL, IN_WIDTH), D_MODEL ** -0.5),
        "sb_q_norm_w": gain(ks[3], (DEPTH, SB_HEAD_DIM)),
        "sb_k_norm_w": gain(ks[4], (DEPTH, SB_HEAD_DIM)),
        "gdn_conv_w": nrm(ks[5], (DEPTH, GDN_CONV, 3 * GDN_WIDTH), GDN_CONV ** -0.5),
        "gdn_a_log": jnp.log(jax.random.uniform(ks[6], (DEPTH, GDN_HEADS), f32, 1.0, 16.0)),
        "gdn_dt_bias": dt + jnp.log(-jnp.expm1(-dt)),
        "gdn_out_norm_w": gain(ks[8], (DEPTH, GDN_HEAD_DIM)),
        "w_branch_sb": nrm(ks[9], (DEPTH, SB_WIDTH, D_MODEL), SB_WIDTH ** -0.5),
        "w_branch_gdn": nrm(ks[10], (DEPTH, GDN_WIDTH, D_MODEL), GDN_WIDTH ** -0.5),
        "w_out": nrm(ks[11], (DEPTH, D_MODEL, D_MODEL), D_MODEL ** -0.5),
        "ffn_norm_w": gain(ks[12], (DEPTH, D_MODEL)),
        "peer_w_q": nrm(ks[13], (DEPTH, D_MODEL, PEER_HEADS * PEER_D_KEY), D_MODEL ** -0.5),
        "peer_keys1": nrm(ks[14], (DEPTH, PEER_HEADS, PEER_N_KEYS, PEER_HALF), PEER_HALF ** -0.5),
        "peer_keys2": nrm(ks[15], (DEPTH, PEER_HEADS, PEER_N_KEYS, PEER_HALF), PEER_HALF ** -0.5),
        "peer_u": nrm(ks[16], (DEPTH, PEER_N_EXPERTS, D_MODEL), D_MODEL ** -0.5),
        "peer_v": nrm(ks[17], (DEPTH, PEER_N_EXPERTS, D_MODEL), PEER_TOPK ** -0.5),
    }


def reference(x, mix_norm_w, w_in, sb_q_norm_w, sb_k_norm_w, gdn_conv_w, gdn_a_log, gdn_dt_bias,
              gdn_out_norm_w, w_branch_sb, w_branch_gdn, w_out, ffn_norm_w, peer_w_q, peer_keys1,
              peer_keys2, peer_u, peer_v):
    B, S, D = x.shape
    f32 = jnp.float32
    sizes = [3 * SB_WIDTH, 3 * GDN_WIDTH, GDN_WIDTH, GDN_HEADS, GDN_HEADS, D_MODEL, D_MODEL]
    offsets = np.cumsum(sizes)[:-1].tolist()
    for l in range(DEPTH):
        h = rms_norm(x, mix_norm_w[l])
        proj = jnp.einsum('bsd,de->bse', h, w_in[l])
        sb_qkv, gdn_qkv, gdn_z, gdn_b, gdn_a, gate_sb, gate_gdn = jnp.split(proj, offsets, axis=-1)

        heads_sb = lambda t: t.reshape(B, S, SB_HEADS, SB_HEAD_DIM).transpose(0, 2, 1, 3)
        q, k, v = [heads_sb(t) for t in jnp.split(sb_qkv, 3, axis=-1)]
        q = rms_norm(q, sb_q_norm_w[l])
        k = rms_norm(k, sb_k_norm_w[l])
        o_sb = stick_breaking_attention(q, k, v).transpose(0, 2, 1, 3).reshape(B, S, SB_WIDTH)

        qkv = jax.nn.silu(causal_depthwise_conv(gdn_qkv, gdn_conv_w[l]))
        heads_gdn = lambda t: t.reshape(B, S, GDN_HEADS, GDN_HEAD_DIM).transpose(0, 2, 1, 3)
        gq, gk, gv = [heads_gdn(t) for t in jnp.split(qkv, 3, axis=-1)]
        gq = l2_normalize(gq)
        gk = l2_normalize(gk)
        gv = gv.astype(f32)
        beta = jax.nn.sigmoid(gdn_b.astype(f32)).transpose(0, 2, 1)
        g = -(jnp.exp(gdn_a_log[l].astype(f32))
              * jax.nn.softplus(gdn_a.astype(f32) + gdn_dt_bias[l].astype(f32))).transpose(0, 2, 1)
        o_gdn = gated_delta_rule_chunked(gq, gk, gv, g, beta).transpose(0, 2, 1, 3)
        z = gdn_z.reshape(B, S, GDN_HEADS, GDN_HEAD_DIM).astype(f32)
        o_gdn = (rms_norm(o_gdn, gdn_out_norm_w[l]) * jax.nn.silu(z)).reshape(B, S, GDN_WIDTH).astype(x.dtype)

        y_sb = jnp.einsum('bse,ed->bsd', o_sb, w_branch_sb[l])
        y_gdn = jnp.einsum('bse,ed->bsd', o_gdn, w_branch_gdn[l])
        merged = jax.nn.sigmoid(gate_sb) * y_sb + jax.nn.sigmoid(gate_gdn) * y_gdn
        x = x + jnp.einsum('bsd,de->bse', merged, w_out[l])

        h2 = rms_norm(x, ffn_norm_w[l])
        x = x + peer_ffn(h2, peer_w_q[l], peer_keys1[l], peer_keys2[l], peer_u[l], peer_v[l])
    return x
```

```python
import functools

import jax
import jax.numpy as jnp
import numpy as np
from jax import lax
from jax.experimental import pallas as pl
from jax.experimental.pallas import tpu as pltpu

F32 = jnp.float32
BF16 = jnp.bfloat16

D_MODEL = 1024
HEADS = 8
HEAD_DIM = 128
LANES = 128
GDN_CONV = 4
GDN_CHUNK = 64
PEER_HEADS = 8
PEER_N_KEYS = 128
PEER_HALF = 128
PEER_TOPK = 16
EPS = 1e-6
MAIN_WIDTH = 9 * D_MODEL
VMEM_LIMIT = 56 * 1024 * 1024
NEG_INF = float("-inf")
POS_INF = float("inf")


def _sigmoid(x):
    return 1.0 / (1.0 + jnp.exp(-x))


def _softplus(x):
    return jnp.maximum(x, 0.0) + jnp.log1p(jnp.exp(-jnp.abs(x)))


def _rms(x, w):
    return x * lax.rsqrt(jnp.mean(x * x, axis=-1, keepdims=True) + EPS) * w


def _dot(a, b):
    return jnp.dot(a, b, preferred_element_type=F32)


def _dot_nt(a, b):
    return lax.dot_general(a, b, (((1,), (1,)), ((), ())), preferred_element_type=F32)


def _dot_tn(a, b):
    return lax.dot_general(a, b, (((0,), (0,)), ((), ())), preferred_element_type=F32)


def _dot_hi(a, b):
    return jnp.dot(a, b, preferred_element_type=F32, precision=lax.Precision.HIGHEST)


def _inproj_kernel(x_ref, nw_ref, w_ref, ws_ref, o_ref, os_ref, h_scr):
    j = pl.program_id(1)

    @pl.when(j == 0)
    def _():
        h = _rms(x_ref[...], nw_ref[...]).astype(BF16)
        h_scr[...] = h
        os_ref[...] = _dot(h, ws_ref[...])

    o_ref[...] = _dot(h_scr[...], w_ref[...]).astype(o_ref.dtype)


def _in_proj(x2d, norm_w, w_main, w_small, tm, tn):
    n = x2d.shape[0]
    return pl.pallas_call(
        _inproj_kernel,
        out_shape=(jax.ShapeDtypeStruct((n, MAIN_WIDTH), BF16),
                   jax.ShapeDtypeStruct((n, LANES), F32)),
        grid=(n // tm, MAIN_WIDTH // tn),
        in_specs=[
            pl.BlockSpec((tm, D_MODEL), lambda i, j: (i, 0)),
            pl.BlockSpec((1, D_MODEL), lambda i, j: (0, 0)),
            pl.BlockSpec((D_MODEL, tn), lambda i, j: (0, j)),
            pl.BlockSpec((D_MODEL, LANES), lambda i, j: (0, 0)),
        ],
        out_specs=(pl.BlockSpec((tm, tn), lambda i, j: (i, j)),
                   pl.BlockSpec((tm, LANES), lambda i, j: (i, 0))),
        scratch_shapes=[pltpu.VMEM((tm, D_MODEL), BF16)],
        compiler_params=pltpu.CompilerParams(
            dimension_semantics=("parallel", "arbitrary"), vmem_limit_bytes=VMEM_LIMIT),
        name="in_proj",
    )(x2d, norm_w, w_main, w_small)


def _sb_kernel(q_ref, k_ref, v_ref, qw_ref, kw_ref, tri_ref, o_ref, kn_scr, *, tq):
    i = pl.program_id(2)
    kb = LANES

    @pl.when(i == 0)
    def _():
        kn_scr[...] = _rms(k_ref[0].astype(F32), kw_ref[...]).astype(BF16)

    scale = HEAD_DIM ** -0.5
    qn = (_rms(q_ref[0].astype(F32), qw_ref[...]) * scale).astype(BF16)
    row = i * tq + lax.broadcasted_iota(jnp.int32, (tq, kb), 0)
    col0 = lax.broadcasted_iota(jnp.int32, (tq, kb), 1)
    nblk = (i + 1) * (tq // kb)

    def body(jj, carry):
        c, acc = carry
        j = nblk - 1 - jj
        k0 = pl.multiple_of(j * kb, kb)
        z = _dot_nt(qn, kn_scr[pl.ds(k0, kb), :])
        causal = (col0 + k0) < row
        sp = _softplus(z)
        lneg = jnp.where(causal, -sp, 0.0)
        hi = lneg.astype(BF16)
        lo = (lneg - hi.astype(F32)).astype(BF16)
        res = _dot(jnp.concatenate([hi, lo], axis=1), tri_ref[...])
        within = res[:, :kb]
        total = res[:, kb:]
        a = jnp.where(causal, jnp.exp((z - sp) + within + c), 0.0)
        acc = acc + _dot(a.astype(BF16), v_ref[0, pl.ds(k0, kb), :])
        return c + total, acc

    init = (jnp.zeros((tq, kb), F32), jnp.zeros((tq, HEAD_DIM), F32))
    _, acc = lax.fori_loop(0, nblk, body, init)
    o_ref[0] = acc.astype(o_ref.dtype)


def _sb_tri():
    r = np.arange(2 * LANES)[:, None] % LANES
    c = np.arange(2 * LANES)[None, :]
    m = np.where(c < LANES, r > c, True)
    return jnp.asarray(m, dtype=BF16)


def _sb_attention(proj3, qw, kw, tq):
    b, s, _ = proj3.shape
    return pl.pallas_call(
        functools.partial(_sb_kernel, tq=tq),
        out_shape=jax.ShapeDtypeStruct((b, s, HEADS * HEAD_DIM), BF16),
        grid=(b, HEADS, s // tq),
        in_specs=[
            pl.BlockSpec((1, tq, HEAD_DIM), lambda bi, h, i: (bi, i, h)),
            pl.BlockSpec((1, s, HEAD_DIM), lambda bi, h, i: (bi, 0, HEADS + h)),
            pl.BlockSpec((1, s, HEAD_DIM), lambda bi, h, i: (bi, 0, 2 * HEADS + h)),
            pl.BlockSpec((1, HEAD_DIM), lambda bi, h, i: (0, 0)),
            pl.BlockSpec((1, HEAD_DIM), lambda bi, h, i: (0, 0)),
            pl.BlockSpec((2 * LANES, 2 * LANES), lambda bi, h, i: (0, 0)),
        ],
        out_specs=pl.BlockSpec((1, tq, HEAD_DIM), lambda bi, h, i: (bi, i, h)),
        scratch_shapes=[pltpu.VMEM((s, HEAD_DIM), BF16)],
        compiler_params=pltpu.CompilerParams(
            dimension_semantics=("parallel", "parallel", "arbitrary"),
            vmem_limit_bytes=VMEM_LIMIT),
        name="sb_attn",
    )(proj3, proj3, proj3, qw, kw, _sb_tri())


def _gdn_kernel(alog_ref, dtb_ref, gq_ref, gk_ref, gv_ref, z_ref, cwq_ref, cwk_ref, cwv_ref,
                acol_ref, bcol_ref, arow_ref, onw_ref, o_ref, xpad, qs, ks, vs):
    h = pl.program_id(1)
    s = qs.shape[0]
    c = GDN_CHUNK
    dk = HEAD_DIM
    pad = 8

    def conv_silu(src_ref, cw_ref):
        xpad[pl.ds(0, pad), :] = jnp.zeros((pad, dk), F32)
        xpad[pl.ds(pad, s), :] = src_ref[0].astype(F32)
        y = cw_ref[0:1, :] * xpad[pl.ds(pad - 3, s), :]
        for t in range(1, GDN_CONV):
            y = y + cw_ref[t:t + 1, :] * xpad[pl.ds(pad - 3 + t, s), :]
        return y * _sigmoid(y)

    def l2n(x):
        return x * lax.rsqrt(jnp.sum(x * x, axis=-1, keepdims=True) + EPS)

    qs[...] = l2n(conv_silu(gq_ref, cwq_ref)) * (dk ** -0.5)
    ks[...] = l2n(conv_silu(gk_ref, cwk_ref))
    vs[...] = conv_silu(gv_ref, cwv_ref)

    a_gain = jnp.exp(jnp.full((1, 1), alog_ref[h], F32))
    dtb = jnp.full((1, 1), dtb_ref[h], F32)

    ri = lax.broadcasted_iota(jnp.int32, (c, c), 0)
    ci = lax.broadcasted_iota(jnp.int32, (c, c), 1)
    tril = ri >= ci
    strict = ri > ci
    tril_f = tril.astype(F32)
    triu_f = (ri <= ci).astype(F32)
    eye = (ri == ci).astype(F32)

    def chunk(n, state):
        r0 = pl.multiple_of(n * c, c)
        q = qs[pl.ds(r0, c), :]
        k = ks[pl.ds(r0, c), :]
        v = vs[pl.ds(r0, c), :]
        g_col = -a_gain * _softplus(acol_ref[0, 0, n] + dtb)
        g_row = -a_gain * _softplus(arow_ref[0, 0, n] + dtb)
        beta = jnp.broadcast_to(_sigmoid(bcol_ref[0, 0, n]), (c, dk))
        gc = _dot_hi(tril_f, jnp.broadcast_to(g_col, (c, dk)))
        gc_row = _dot_hi(jnp.broadcast_to(g_row, (8, c)), triu_f)[0:1]
        gdiff = gc[:, :c] - gc_row
        decay = jnp.where(tril, jnp.exp(jnp.where(tril, gdiff, 0.0)), 0.0)

        kbf = k.astype(BF16)
        k_beta = k * beta
        v_beta = v * beta
        lmat = jnp.where(strict, _dot_nt(k_beta.astype(BF16), kbf) * decay, 0.0)
        m = -lmat
        t = eye + m
        for _ in range(5):
            m = _dot_hi(m, m)
            t = t + _dot_hi(t, m)
        tb = t.astype(BF16)
        egc = jnp.exp(gc)
        v_corr = _dot(tb, v_beta.astype(BF16))
        k_cd = _dot(tb, (k_beta * egc).astype(BF16))
        attn = jnp.where(tril, _dot_nt(q.astype(BF16), kbf) * decay, 0.0)

        sb = state.astype(BF16)
        v_new = v_corr - _dot(k_cd.astype(BF16), sb)
        vnb = v_new.astype(BF16)
        o = _dot((q * egc).astype(BF16), sb) + _dot(attn.astype(BF16), vnb)
        g_last = gc[c - 1:c, :]
        k_w = k * jnp.exp(g_last - gc)
        state = state * jnp.exp(g_last) + _dot_tn(k_w.astype(BF16), vnb)

        zc = z_ref[0, pl.ds(r0, c), :].astype(F32)
        o_ref[0, pl.ds(r0, c), :] = (_rms(o, onw_ref[...]) * (zc * _sigmoid(zc))).astype(o_ref.dtype)
        return state

    lax.fori_loop(0, s // c, chunk, jnp.zeros((dk, dk), F32))


def _gdn(proj3, conv_w, a_log, dt_bias, a_col, b_col, a_row, out_norm_w):
    b, s, _ = proj3.shape
    n = s // GDN_CHUNK
    c = GDN_CHUNK
    col = lambda base: pl.BlockSpec((1, s, HEAD_DIM), lambda bi, h: (bi, 0, base + h))
    cw = lambda base: pl.BlockSpec((GDN_CONV, HEAD_DIM), lambda bi, h: (0, base + h))
    smem = pl.BlockSpec(memory_space=pltpu.SMEM)
    return pl.pallas_call(
        _gdn_kernel,
        out_shape=jax.ShapeDtypeStruct((b, s, HEADS * HEAD_DIM), BF16),
        grid=(b, HEADS),
        in_specs=[
            smem, smem,
            col(3 * HEADS), col(4 * HEADS), col(5 * HEADS), col(6 * HEADS),
            cw(0), cw(HEADS), cw(2 * HEADS),
            pl.BlockSpec((1, 1, n, c, 1), lambda bi, h: (bi, h, 0, 0, 0)),
            pl.BlockSpec((1, 1, n, c, 1), lambda bi, h: (bi, h, 0, 0, 0)),
            pl.BlockSpec((1, 1, n, 1, c), lambda bi, h: (bi, h, 0, 0, 0)),
            pl.BlockSpec((1, HEAD_DIM), lambda bi, h: (0, 0)),
        ],
        out_specs=pl.BlockSpec((1, s, HEAD_DIM), lambda bi, h: (bi, 0, h)),
        scratch_shapes=[pltpu.VMEM((s + 8, HEAD_DIM), F32)] + [pltpu.VMEM((s, HEAD_DIM), F32)] * 3,
        compiler_params=pltpu.CompilerParams(
            dimension_semantics=("parallel", "parallel"), vmem_limit_bytes=VMEM_LIMIT),
        name="gdn",
    )(a_log, dt_bias, proj3, proj3, proj3, proj3, conv_w, conv_w, conv_w,
      a_col, b_col, a_row, out_norm_w)


def _merge_kernel(osb_ref, ogdn_ref, gsb_ref, ggdn_ref, x_ref, wsb_ref, wgdn_ref, wout_ref,
                  fnw_ref, wq_ref, k1_ref, k2_ref, x1_ref, h2_ref, s1_ref, s2_ref):
    y_sb = _dot(osb_ref[...], wsb_ref[...])
    y_gdn = _dot(ogdn_ref[...], wgdn_ref[...])
    merged = (_sigmoid(gsb_ref[...].astype(F32)) * y_sb
              + _sigmoid(ggdn_ref[...].astype(F32)) * y_gdn)
    x1 = x_ref[...] + _dot(merged.astype(BF16), wout_ref[...])
    x1_ref[...] = x1
    h2 = _rms(x1, fnw_ref[...]).astype(BF16)
    h2_ref[...] = h2
    q = _dot(h2, wq_ref[...]).astype(BF16)
    for hh in range(PEER_HEADS):
        base = hh * 2 * PEER_HALF
        s1_ref[hh] = _dot_nt(k1_ref[hh], q[:, base:base + PEER_HALF])
        s2_ref[hh] = _dot_nt(k2_ref[hh], q[:, base + PEER_HALF:base + 2 * PEER_HALF])


def _merge(o_sb, o_gdn, proj_main, x2d, w_sb, w_gdn, w_out, ffn_w, w_q, keys1, keys2, tm):
    n = x2d.shape[0]
    full = lambda shape: pl.BlockSpec(shape, lambda i: (0,) * len(shape))
    tok = lambda width, cb=0: pl.BlockSpec((tm, width), lambda i, cb=cb: (i, cb))
    st_spec = pl.BlockSpec((PEER_HEADS, PEER_N_KEYS, tm), lambda i: (0, 0, i))
    return pl.pallas_call(
        _merge_kernel,
        out_shape=(jax.ShapeDtypeStruct((n, D_MODEL), F32),
                   jax.ShapeDtypeStruct((n, D_MODEL), BF16),
                   jax.ShapeDtypeStruct((PEER_HEADS, PEER_N_KEYS, n), F32),
                   jax.ShapeDtypeStruct((PEER_HEADS, PEER_N_KEYS, n), F32)),
        grid=(n // tm,),
        in_specs=[
            tok(D_MODEL), tok(D_MODEL), tok(D_MODEL, 7), tok(D_MODEL, 8), tok(D_MODEL),
            full((D_MODEL, D_MODEL)), full((D_MODEL, D_MODEL)), full((D_MODEL, D_MODEL)),
            full((1, D_MODEL)), full((D_MODEL, 2 * PEER_HALF * PEER_HEADS)),
            full((PEER_HEADS, PEER_N_KEYS, PEER_HALF)), full((PEER_HEADS, PEER_N_KEYS, PEER_HALF)),
        ],
        out_specs=(tok(D_MODEL), tok(D_MODEL), st_spec, st_spec),
        compiler_params=pltpu.CompilerParams(
            dimension_semantics=("parallel",), vmem_limit_bytes=VMEM_LIMIT),
        name="merge",
    )(o_sb, o_gdn, proj_main, proj_main, x2d, w_sb, w_gdn, w_out, ffn_w, w_q, keys1, keys2)


def _top_values(s, k):
    vals = []
    for _ in range(k):
        m = jnp.max(s, axis=0, keepdims=True)
        vals.append(m)
        s = jnp.where(s == m, NEG_INF, s)
    return jnp.concatenate(vals, axis=0)


def _topk_kernel(s1_ref, s2_ref, c_ref, p1_ref, p2_ref, *, tl):
    kk = PEER_TOPK

    def sub(bi, _):
        l0 = pl.multiple_of(bi * LANES, LANES)
        s1 = s1_ref[0, :, pl.ds(l0, LANES)]
        s2 = s2_ref[0, :, pl.ds(l0, LANES)]
        v1 = _top_values(s1, kk)
        v2 = _top_values(s2, kk)
        jrow8 = lax.broadcasted_iota(jnp.int32, (8, LANES), 0)
        blocks = [(v1[0:1] + v2, v2)]
        for i in range(1, 8):
            lim = kk // (i + 1)
            vv = jnp.where(jrow8 < lim, v2[0:8], NEG_INF)
            blocks.append((v1[i:i + 1] + vv, v2[0:8]))
        blocks.append((v1[8:16] + v2[0:1], None))
        cand = jnp.concatenate([b[0] for b in blocks], axis=0)
        top = cand[0:1]
        work = cand
        tau = top
        for _ in range(kk):
            tau = jnp.max(work, axis=0, keepdims=True)
            work = jnp.where(work == tau, NEG_INF, work)
        sel = cand >= tau
        zsum = jnp.sum(jnp.where(sel, jnp.exp(jnp.where(sel, cand - top, 0.0)), 0.0),
                       axis=0, keepdims=True)
        crows = []
        for i in range(8):
            cb, vb = blocks[i]
            crows.append(jnp.min(jnp.where(cb >= tau, vb, POS_INF), axis=0, keepdims=True))
        crows.append(jnp.where(blocks[8][0] >= tau, v2[0:1], POS_INF))
        crank = jnp.concatenate(crows, axis=0)
        cfull = jnp.full(s1.shape, POS_INF, F32)
        for i in range(kk):
            cfull = jnp.where(s1 == v1[i:i + 1], crank[i:i + 1], cfull)
        c_ref[0, :, pl.ds(l0, LANES)] = cfull
        p1_ref[0, :, pl.ds(l0, LANES)] = jnp.exp(s1 - v1[0:1]) / zsum
        p2_ref[0, :, pl.ds(l0, LANES)] = jnp.exp(s2 - v2[0:1])
        return 0

    lax.fori_loop(0, tl // LANES, sub, 0)


def _peer_topk(s1t, s2t, tl):
    hh, kk, n = s1t.shape
    spec = pl.BlockSpec((1, kk, tl), lambda i, h: (h, 0, i))
    shp = jax.ShapeDtypeStruct((hh, kk, n), F32)
    return pl.pallas_call(
        functools.partial(_topk_kernel, tl=tl),
        out_shape=(shp, shp, shp),
        grid=(n // tl, hh),
        in_specs=[spec, spec],
        out_specs=(spec, spec, spec),
        compiler_params=pltpu.CompilerParams(
            dimension_semantics=("parallel", "parallel"), vmem_limit_bytes=VMEM_LIMIT),
        name="peer_topk",
    )(s1t, s2t)


def _dense_kernel(h2t_ref, u_ref, vt_ref, s2_ref, p2_ref, c_ref, p1_ref, x1_ref, o_ref,
                  acc_scr, w_scr, *, ec):
    j = pl.program_id(1)
    nk = PEER_N_KEYS

    @pl.when(j == 0)
    def _():
        acc_scr[...] = jnp.zeros_like(acc_scr)

    pre = _dot(u_ref[...], h2t_ref[...])
    for e in range(ec // nk):
        e1 = j * (ec // nk) + e
        blk = pre[e * nk:(e + 1) * nk]
        act = 0.5 * blk * (1.0 + lax.erf(blk * (2.0 ** -0.5)))
        g = None
        for hh in range(PEER_HEADS):
            thr = c_ref[hh, pl.ds(e1, 1), :]
            p1 = p1_ref[hh, pl.ds(e1, 1), :]
            term = jnp.where(s2_ref[hh] >= thr, p2_ref[hh], 0.0) * p1
            g = term if g is None else g + term
        w_scr[pl.ds(e * nk, nk), :] = (act * g).astype(BF16)
    acc_scr[...] += _dot(vt_ref[...], w_scr[...])

    @pl.when(j == pl.num_programs(1) - 1)
    def _():
        o_ref[...] = x1_ref[...] + acc_scr[...].T


def _peer_dense(h2t, u_tab, vt_tab, s2t, p2, cthr, p1, x1, tl, ec):
    d, n = h2t.shape
    ne = u_tab.shape[0]
    hspec = pl.BlockSpec((PEER_HEADS, PEER_N_KEYS, tl), lambda i, j: (0, 0, i))
    return pl.pallas_call(
        functools.partial(_dense_kernel, ec=ec),
        out_shape=jax.ShapeDtypeStruct((n, d), F32),
        grid=(n // tl, ne // ec),
        in_specs=[
            pl.BlockSpec((d, tl), lambda i, j: (0, i)),
            pl.BlockSpec((ec, d), lambda i, j: (j, 0)),
            pl.BlockSpec((d, ec), lambda i, j: (0, j)),
            hspec, hspec, hspec, hspec,
            pl.BlockSpec((tl, d), lambda i, j: (i, 0)),
        ],
        out_specs=pl.BlockSpec((tl, d), lambda i, j: (i, 0)),
        scratch_shapes=[pltpu.VMEM((d, tl), F32), pltpu.VMEM((ec, tl), BF16)],
        compiler_params=pltpu.CompilerParams(
            dimension_semantics=("parallel", "arbitrary"), vmem_limit_bytes=VMEM_LIMIT),
        name="peer_dense",
    )(h2t, u_tab, vt_tab, s2t, p2, cthr, p1, x1)


def _tile(n, pref):
    t = min(n, pref)
    assert n % t == 0, (n, t)
    return t


def kernel(x, mix_norm_w, w_in, sb_q_norm_w, sb_k_norm_w, gdn_conv_w, gdn_a_log, gdn_dt_bias,
           gdn_out_norm_w, w_branch_sb, w_branch_gdn, w_out, ffn_norm_w, peer_w_q, peer_keys1,
           peer_keys2, peer_u, peer_v):
    b, s, d = x.shape
    n = b * s
    depth = w_in.shape[0]
    sb_w = HEADS * HEAD_DIM
    small0 = 7 * sb_w
    small1 = small0 + 2 * HEADS
    nchunk = s // GDN_CHUNK
    x2d = x.reshape(n, d)
    for l in range(depth):
        w_main = jnp.concatenate([w_in[l][:, :small0], w_in[l][:, small1:]], axis=1).astype(BF16)
        w_small = jnp.pad(w_in[l][:, small0:small1], ((0, 0), (0, LANES - 2 * HEADS))).astype(BF16)
        proj_main, proj_small = _in_proj(x2d, mix_norm_w[l][None], w_main, w_small,
                                         _tile(n, 1024), 1024)
        proj3 = proj_main.reshape(b, s, MAIN_WIDTH)

        o_sb = _sb_attention(proj3, sb_q_norm_w[l][None], sb_k_norm_w[l][None], _tile(s, 256))

        ba = proj_small[:, :2 * HEADS].reshape(b, nchunk, GDN_CHUNK, 2 * HEADS)
        ba = ba.transpose(0, 3, 1, 2)
        b_col = ba[:, :HEADS, :, :, None]
        a_col = ba[:, HEADS:, :, :, None]
        a_row = ba[:, HEADS:, :, None, :]
        o_gdn = _gdn(proj3, gdn_conv_w[l], gdn_a_log[l], gdn_dt_bias[l], a_col, b_col, a_row,
                     gdn_out_norm_w[l][None])

        x1, h2, s1t, s2t = _merge(
            o_sb.reshape(n, sb_w), o_gdn.reshape(n, sb_w), proj_main, x2d,
            w_branch_sb[l].astype(BF16), w_branch_gdn[l].astype(BF16), w_out[l].astype(BF16),
            ffn_norm_w[l][None], peer_w_q[l].astype(BF16),
            peer_keys1[l].astype(BF16), peer_keys2[l].astype(BF16), _tile(n, 256))

        cthr, p1, p2 = _peer_topk(s1t, s2t, _tile(n, 1024))
        x2d = _peer_dense(h2.T, peer_u[l].astype(BF16), peer_v[l].astype(BF16).T,
                          s2t, p2, cthr, p1, x1, _tile(n, 512), 512)
    return x2d.reshape(b, s, d)
```

```python
import functools

import jax
import jax.numpy as jnp
import numpy as np
from jax import lax
from jax.experimental import pallas as pl
from jax.experimental.pallas import tpu as pltpu

F32 = jnp.float32
BF16 = jnp.bfloat16

D_MODEL = 1024
HEADS = 8
HEAD_DIM = 128
LANES = 128
GDN_CONV = 4
GDN_CHUNK = 64
PEER_HEADS = 8
PEER_N_KEYS = 128
PEER_HALF = 128
PEER_TOPK = 16
EPS = 1e-6
MAIN_WIDTH = 9 * D_MODEL
VMEM_LIMIT = 56 * 1024 * 1024
SB_LOG_UNDERFLOW = -104.0
SB_GROUP = 2
GDN_GROUP = 2
GDN_UNROLL = 2
NEG_INF = float("-inf")
POS_INF = float("inf")


def _sigmoid(x):
    return 1.0 / (1.0 + jnp.exp(-x))


def _softplus(x):
    return jnp.maximum(x, 0.0) + jnp.log1p(jnp.exp(-jnp.abs(x)))


def _rms(x, w):
    return x * lax.rsqrt(jnp.mean(x * x, axis=-1, keepdims=True) + EPS) * w


def _dot(a, b):
    return jnp.dot(a, b, preferred_element_type=F32)


def _dot_nt(a, b):
    return lax.dot_general(a, b, (((1,), (1,)), ((), ())), preferred_element_type=F32)


def _dot_tn(a, b):
    return lax.dot_general(a, b, (((0,), (0,)), ((), ())), preferred_element_type=F32)


def _dot_hi(a, b):
    return jnp.dot(a, b, preferred_element_type=F32, precision=lax.Precision.HIGHEST)


def _inproj_kernel(x_ref, nw_ref, w_ref, ws_ref, o_ref, os_ref, h_scr):
    j = pl.program_id(1)

    @pl.when(j == 0)
    def _():
        h = _rms(x_ref[...], nw_ref[...]).astype(BF16)
        h_scr[...] = h
        os_ref[...] = _dot(h, ws_ref[...])

    o_ref[...] = _dot(h_scr[...], w_ref[...]).astype(o_ref.dtype)


def _in_proj(x2d, norm_w, w_main, w_small, tm, tn):
    n = x2d.shape[0]
    return pl.pallas_call(
        _inproj_kernel,
        out_shape=(jax.ShapeDtypeStruct((n, MAIN_WIDTH), BF16),
                   jax.ShapeDtypeStruct((n, LANES), F32)),
        grid=(n // tm, MAIN_WIDTH // tn),
        in_specs=[
            pl.BlockSpec((tm, D_MODEL), lambda i, j: (i, 0)),
            pl.BlockSpec((1, D_MODEL), lambda i, j: (0, 0)),
            pl.BlockSpec((D_MODEL, tn), lambda i, j: (0, j)),
            pl.BlockSpec((D_MODEL, LANES), lambda i, j: (0, 0)),
        ],
        out_specs=(pl.BlockSpec((tm, tn), lambda i, j: (i, j)),
                   pl.BlockSpec((tm, LANES), lambda i, j: (i, 0))),
        scratch_shapes=[pltpu.VMEM((tm, D_MODEL), BF16)],
        compiler_params=pltpu.CompilerParams(
            dimension_semantics=("parallel", "arbitrary"), vmem_limit_bytes=VMEM_LIMIT),
        name="in_proj",
    )(x2d, norm_w, w_main, w_small)


def _sb_kernel(q_ref, k_ref, v_ref, qw_ref, kw_ref, tri_ref, o_ref, kn_scr, *, tq):
    i = pl.program_id(2)
    kb = LANES
    dh = HEAD_DIM
    heads = range(SB_GROUP)

    @pl.when(i == 0)
    def _():
        for g in heads:
            kn_scr[:, g * dh:(g + 1) * dh] = _rms(
                k_ref[0, :, g * dh:(g + 1) * dh].astype(F32), kw_ref[...]).astype(BF16)

    scale = dh ** -0.5
    qn = [(_rms(q_ref[0, :, g * dh:(g + 1) * dh].astype(F32), qw_ref[...]) * scale).astype(BF16)
          for g in heads]
    row = i * tq + lax.broadcasted_iota(jnp.int32, (tq, kb), 0)
    col0 = lax.broadcasted_iota(jnp.int32, (tq, kb), 1)
    npair = (i + 1) * (tq // (2 * kb))

    def cond(carry):
        p, cs, _ = carry
        cmax = cs[0]
        for g in heads[1:]:
            cmax = jnp.maximum(cmax, cs[g])
        return jnp.logical_and(p < npair, jnp.max(cmax) > SB_LOG_UNDERFLOW)

    def body(carry):
        p, cs, accs = carry
        k0 = pl.multiple_of((npair - 1 - p) * (2 * kb), 2 * kb)
        masks = [(col0 + (k0 + half * kb)) < row for half in range(2)]
        z2 = [_dot_nt(qn[g], kn_scr[pl.ds(k0, 2 * kb), g * dh:(g + 1) * dh]) for g in heads]
        cats, logits = [], []
        for g in heads:
            cat_g, logit_g = [], []
            for half in range(2):
                z = z2[g][:, half * kb:(half + 1) * kb]
                sp = jnp.maximum(z, 0.0) + jnp.log(1.0 + jnp.exp(-jnp.abs(z)))
                lneg = jnp.where(masks[half], -sp, 0.0)
                hi = lneg.astype(BF16)
                lo = (lneg - hi.astype(F32)).astype(BF16)
                cat_g.append(jnp.concatenate([hi, lo], axis=1))
                logit_g.append(z - sp)
            cats.append(jnp.concatenate(cat_g, axis=0))
            logits.append(logit_g)
        res = [_dot(cats[g], tri_ref[...]) for g in heads]
        a2, c_new = [], []
        for g in heads:
            c_early = cs[g] + res[g][tq:, kb:]
            a_late = jnp.where(masks[1], jnp.exp(logits[g][1] + res[g][tq:, :kb] + cs[g]), 0.0)
            a_early = jnp.where(masks[0], jnp.exp(logits[g][0] + res[g][:tq, :kb] + c_early), 0.0)
            a2.append(jnp.concatenate([a_early, a_late], axis=1).astype(BF16))
            c_new.append(c_early + res[g][:tq, kb:])
        accs = tuple(accs[g] + _dot(a2[g], v_ref[0, pl.ds(k0, 2 * kb), g * dh:(g + 1) * dh])
                     for g in heads)
        return p + 1, tuple(c_new), accs

    init = (jnp.int32(0), tuple(jnp.zeros((tq, kb), F32) for _ in heads),
            tuple(jnp.zeros((tq, dh), F32) for _ in heads))
    _, _, accs = lax.while_loop(cond, body, init)
    for g in heads:
        o_ref[0, :, g * dh:(g + 1) * dh] = accs[g].astype(o_ref.dtype)


def _sb_tri():
    r = np.arange(2 * LANES)[:, None] % LANES
    c = np.arange(2 * LANES)[None, :]
    m = np.where(c < LANES, r > c, True)
    return jnp.asarray(m, dtype=BF16)


def _sb_attention(proj3, qw, kw, tq):
    b, s, _ = proj3.shape
    w = SB_GROUP * HEAD_DIM
    ng = HEADS // SB_GROUP
    return pl.pallas_call(
        functools.partial(_sb_kernel, tq=tq),
        out_shape=jax.ShapeDtypeStruct((b, s, HEADS * HEAD_DIM), BF16),
        grid=(b, ng, s // tq),
        in_specs=[
            pl.BlockSpec((1, tq, w), lambda bi, h, i: (bi, i, h)),
            pl.BlockSpec((1, s, w), lambda bi, h, i: (bi, 0, ng + h)),
            pl.BlockSpec((1, s, w), lambda bi, h, i: (bi, 0, 2 * ng + h)),
            pl.BlockSpec((1, HEAD_DIM), lambda bi, h, i: (0, 0)),
            pl.BlockSpec((1, HEAD_DIM), lambda bi, h, i: (0, 0)),
            pl.BlockSpec((2 * LANES, 2 * LANES), lambda bi, h, i: (0, 0)),
        ],
        out_specs=pl.BlockSpec((1, tq, w), lambda bi, h, i: (bi, i, h)),
        scratch_shapes=[pltpu.VMEM((s, w), BF16)],
        compiler_params=pltpu.CompilerParams(
            dimension_semantics=("parallel", "parallel", "arbitrary"),
            vmem_limit_bytes=VMEM_LIMIT),
        name="sb_attn",
    )(proj3, proj3, proj3, qw, kw, _sb_tri())


def _gdn_kernel(alog_ref, dtb_ref, gq_ref, gk_ref, gv_ref, z_ref, cwq_ref, cwk_ref, cwv_ref,
                acol_ref, bcol_ref, arow_ref, onw_ref, o_ref,
                xpad, qs, ks, vs, m_scr, b_scr, qp_scr, op_scr, eg_scr):
    hg = pl.program_id(1)
    s = qs.shape[0]
    c = GDN_CHUNK
    dk = HEAD_DIM
    w = GDN_GROUP * dk
    pad = 8

    def conv_silu(src_ref, cw_ref):
        xpad[pl.ds(0, pad), :] = jnp.zeros((pad, w), F32)
        xpad[pl.ds(pad, s), :] = src_ref[0].astype(F32)
        y = cw_ref[0:1, :] * xpad[pl.ds(pad - 3, s), :]
        for t in range(1, GDN_CONV):
            y = y + cw_ref[t:t + 1, :] * xpad[pl.ds(pad - 3 + t, s), :]
        return y * _sigmoid(y)

    def l2n(x):
        return x * lax.rsqrt(jnp.sum(x * x, axis=-1, keepdims=True) + EPS)

    yq = conv_silu(gq_ref, cwq_ref)
    for g in range(GDN_GROUP):
        qs[:, g * dk:(g + 1) * dk] = l2n(yq[:, g * dk:(g + 1) * dk]) * (dk ** -0.5)
    yk = conv_silu(gk_ref, cwk_ref)
    for g in range(GDN_GROUP):
        ks[:, g * dk:(g + 1) * dk] = l2n(yk[:, g * dk:(g + 1) * dk])
    vs[...] = conv_silu(gv_ref, cwv_ref)

    a_gain = [jnp.exp(jnp.full((1, 1), alog_ref[hg * GDN_GROUP + g], F32)) for g in range(GDN_GROUP)]
    dtb = [jnp.full((1, 1), dtb_ref[hg * GDN_GROUP + g], F32) for g in range(GDN_GROUP)]

    ri = lax.broadcasted_iota(jnp.int32, (c, c), 0)
    ci = lax.broadcasted_iota(jnp.int32, (c, c), 1)
    tril = ri >= ci
    strict = ri > ci
    tril_b = jnp.where(tril, 1.0, 0.0).astype(BF16)
    triu_b = jnp.where(ri <= ci, 1.0, 0.0).astype(BF16)
    eye = jnp.where(ri == ci, 1.0, 0.0)

    def split(x):
        hi = x.astype(BF16)
        return hi, (x - hi.astype(F32)).astype(BF16)

    def prepare(nn, carry):
        pairs = [(g, nn * GDN_UNROLL + j) for j in range(GDN_UNROLL) for g in range(GDN_GROUP)]
        ps = range(len(pairs))
        rows = [pl.ds(pl.multiple_of(n * c, c), c) for _, n in pairs]
        q = [qs[rows[i], g * dk:(g + 1) * dk] for i, (g, _) in enumerate(pairs)]
        k = [ks[rows[i], g * dk:(g + 1) * dk] for i, (g, _) in enumerate(pairs)]
        v = [vs[rows[i], g * dk:(g + 1) * dk] for i, (g, _) in enumerate(pairs)]
        g_col = [-a_gain[g] * _softplus(acol_ref[0, g, n] + dtb[g]) for g, n in pairs]
        g_row = [-a_gain[g] * _softplus(arow_ref[0, g, n] + dtb[g]) for g, n in pairs]
        beta = [jnp.broadcast_to(_sigmoid(bcol_ref[0, g, n]), (c, dk)) for g, n in pairs]
        gcs = [split(jnp.broadcast_to(g_col[i], (c, dk))) for i in ps]
        gc = [_dot(tril_b, gcs[i][0]) + _dot(tril_b, gcs[i][1]) for i in ps]
        grs = [split(jnp.broadcast_to(g_row[i], (8, c))) for i in ps]
        gc_row = [(_dot(grs[i][0], triu_b) + _dot(grs[i][1], triu_b))[0:1] for i in ps]
        decay = [jnp.where(tril, jnp.exp(jnp.where(tril, gc[i][:, :c] - gc_row[i], 0.0)), 0.0)
                 for i in ps]
        kbf = [k[i].astype(BF16) for i in ps]
        k_beta = [k[i] * beta[i] for i in ps]
        u = [-jnp.where(strict, _dot_nt(k_beta[i].astype(BF16), kbf[i]) * decay[i], 0.0) for i in ps]
        mb = [u[i].astype(BF16) for i in ps]
        for _ in range(5):
            m2 = [_dot(mb[i], mb[i]) for i in ps]
            mb = [m2[i].astype(BF16) for i in ps]
            u = [u[i] + m2[i] + _dot(u[i].astype(BF16), mb[i]) for i in ps]
        tb = [(eye + u[i]).astype(BF16) for i in ps]
        egc = [jnp.exp(gc[i]) for i in ps]
        vc_b = [_dot(tb[i], (v[i] * beta[i]).astype(BF16)).astype(BF16) for i in ps]
        kcd_b = [_dot(tb[i], (k_beta[i] * egc[i]).astype(BF16)).astype(BF16) for i in ps]
        attn_b = [jnp.where(tril, _dot_nt(q[i].astype(BF16), kbf[i]) * decay[i], 0.0).astype(BF16)
                  for i in ps]
        g_last = [gc[i][c - 1:c, :] for i in ps]
        kw_b = [(k[i] * jnp.exp(g_last[i] - gc[i])).astype(BF16) for i in ps]
        m_out = [_dot_tn(kw_b[i], kcd_b[i]).astype(BF16) for i in ps]
        b_out = [_dot_tn(kw_b[i], vc_b[i]) for i in ps]
        qp_out = [(q[i] * egc[i] - _dot(attn_b[i], kcd_b[i])).astype(BF16) for i in ps]
        op_out = [_dot(attn_b[i], vc_b[i]) for i in ps]
        for i, (g, n) in enumerate(pairs):
            m_scr[g, n] = m_out[i]
            b_scr[g, n] = b_out[i]
            qp_scr[g, n] = qp_out[i]
            op_scr[g, n] = op_out[i]
            eg_scr[g, n] = jnp.broadcast_to(jnp.exp(g_last[i]), (8, dk))
        return carry

    lax.fori_loop(0, s // (c * GDN_UNROLL), prepare, 0)

    def scan(n, states):
        r0 = pl.multiple_of(n * c, c)
        gs = range(GDN_GROUP)
        sb = [states[g].astype(BF16) for g in gs]
        zc = [z_ref[0, pl.ds(r0, c), g * dk:(g + 1) * dk].astype(F32) for g in gs]
        o = [_dot(qp_scr[g, n], sb[g]) + op_scr[g, n] for g in gs]
        new = tuple(states[g] * eg_scr[g, n][0:1] - _dot(m_scr[g, n], sb[g]) + b_scr[g, n]
                    for g in gs)
        for g in gs:
            o_ref[0, pl.ds(r0, c), g * dk:(g + 1) * dk] = (
                _rms(o[g], onw_ref[...]) * (zc[g] * _sigmoid(zc[g]))).astype(o_ref.dtype)
        return new

    lax.fori_loop(0, s // c, scan, tuple(jnp.zeros((dk, dk), F32) for _ in range(GDN_GROUP)))


def _gdn(proj3, conv_w, a_log, dt_bias, a_col, b_col, a_row, out_norm_w):
    b, s, _ = proj3.shape
    n = s // GDN_CHUNK
    c = GDN_CHUNK
    gg = GDN_GROUP
    w = gg * HEAD_DIM
    col = lambda base: pl.BlockSpec((1, s, w), lambda bi, h: (bi, 0, base // gg + h))
    cw = lambda base: pl.BlockSpec((GDN_CONV, w), lambda bi, h: (0, base // gg + h))
    smem = pl.BlockSpec(memory_space=pltpu.SMEM)
    return pl.pallas_call(
        _gdn_kernel,
        out_shape=jax.ShapeDtypeStruct((b, s, HEADS * HEAD_DIM), BF16),
        grid=(b, HEADS // gg),
        in_specs=[
            smem, smem,
            col(3 * HEADS), col(4 * HEADS), col(5 * HEADS), col(6 * HEADS),
            cw(0), cw(HEADS), cw(2 * HEADS),
            pl.BlockSpec((1, gg, n, c, 1), lambda bi, h: (bi, h, 0, 0, 0)),
            pl.BlockSpec((1, gg, n, c, 1), lambda bi, h: (bi, h, 0, 0, 0)),
            pl.BlockSpec((1, gg, n, 1, c), lambda bi, h: (bi, h, 0, 0, 0)),
            pl.BlockSpec((1, HEAD_DIM), lambda bi, h: (0, 0)),
        ],
        out_specs=pl.BlockSpec((1, s, w), lambda bi, h: (bi, 0, h)),
        scratch_shapes=[
            pltpu.VMEM((s + 8, w), F32), pltpu.VMEM((s, w), F32), pltpu.VMEM((s, w), F32),
            pltpu.VMEM((s, w), F32),
            pltpu.VMEM((gg, n, HEAD_DIM, HEAD_DIM), BF16), pltpu.VMEM((gg, n, HEAD_DIM, HEAD_DIM), F32),
            pltpu.VMEM((gg, n, c, HEAD_DIM), BF16), pltpu.VMEM((gg, n, c, HEAD_DIM), F32),
            pltpu.VMEM((gg, n, 8, HEAD_DIM), F32),
        ],
        compiler_params=pltpu.CompilerParams(
            dimension_semantics=("parallel", "parallel"), vmem_limit_bytes=VMEM_LIMIT),
        name="gdn",
    )(a_log, dt_bias, proj3, proj3, proj3, proj3, conv_w, conv_w, conv_w,
      a_col, b_col, a_row, out_norm_w)


def _merge_kernel(osb_ref, ogdn_ref, gsb_ref, ggdn_ref, x_ref, wsb_ref, wgdn_ref, wout_ref,
                  fnw_ref, wq_ref, k1_ref, k2_ref, x1_ref, h2_ref, s1_ref, s2_ref):
    y_sb = _dot(osb_ref[...], wsb_ref[...])
    y_gdn = _dot(ogdn_ref[...], wgdn_ref[...])
    merged = (_sigmoid(gsb_ref[...].astype(F32)) * y_sb
              + _sigmoid(ggdn_ref[...].astype(F32)) * y_gdn)
    x1 = x_ref[...] + _dot(merged.astype(BF16), wout_ref[...])
    x1_ref[...] = x1
    h2 = _rms(x1, fnw_ref[...]).astype(BF16)
    h2_ref[...] = h2
    q = _dot(h2, wq_ref[...]).astype(BF16)
    for hh in range(PEER_HEADS):
        base = hh * 2 * PEER_HALF
        s1_ref[hh] = _dot_nt(k1_ref[hh], q[:, base:base + PEER_HALF])
        s2_ref[hh] = _dot_nt(k2_ref[hh], q[:, base + PEER_HALF:base + 2 * PEER_HALF])


def _merge(o_sb, o_gdn, proj_main, x2d, w_sb, w_gdn, w_out, ffn_w, w_q, keys1, keys2, tm):
    n = x2d.shape[0]
    full = lambda shape: pl.BlockSpec(shape, lambda i: (0,) * len(shape))
    tok = lambda width, cb=0: pl.BlockSpec((tm, width), lambda i, cb=cb: (i, cb))
    st_spec = pl.BlockSpec((PEER_HEADS, PEER_N_KEYS, tm), lambda i: (0, 0, i))
    return pl.pallas_call(
        _merge_kernel,
        out_shape=(jax.ShapeDtypeStruct((n, D_MODEL), F32),
                   jax.ShapeDtypeStruct((n, D_MODEL), BF16),
                   jax.ShapeDtypeStruct((PEER_HEADS, PEER_N_KEYS, n), F32),
                   jax.ShapeDtypeStruct((PEER_HEADS, PEER_N_KEYS, n), F32)),
        grid=(n // tm,),
        in_specs=[
            tok(D_MODEL), tok(D_MODEL), tok(D_MODEL, 7), tok(D_MODEL, 8), tok(D_MODEL),
            full((D_MODEL, D_MODEL)), full((D_MODEL, D_MODEL)), full((D_MODEL, D_MODEL)),
            full((1, D_MODEL)), full((D_MODEL, 2 * PEER_HALF * PEER_HEADS)),
            full((PEER_HEADS, PEER_N_KEYS, PEER_HALF)), full((PEER_HEADS, PEER_N_KEYS, PEER_HALF)),
        ],
        out_specs=(tok(D_MODEL), tok(D_MODEL), st_spec, st_spec),
        compiler_params=pltpu.CompilerParams(
            dimension_semantics=("parallel",), vmem_limit_bytes=VMEM_LIMIT),
        name="merge",
    )(o_sb, o_gdn, proj_main, proj_main, x2d, w_sb, w_gdn, w_out, ffn_w, w_q, keys1, keys2)


def _top_ranked(s, k):
    vals = []
    rank = jnp.full(s.shape, float(k), F32)
    for r in range(k):
        m = jnp.max(s, axis=0, keepdims=True)
        vals.append(m)
        hit = s == m
        rank = jnp.where(hit, float(r), rank)
        s = jnp.where(hit, NEG_INF, s)
    return jnp.concatenate(vals, axis=0), rank


def _topk_kernel(s1_ref, s2_ref, n1_ref, p1_ref, r2_ref, p2_ref, *, tl):
    kk = PEER_TOPK

    def sub(bi, _):
        l0 = pl.multiple_of(bi * LANES, LANES)
        s1 = s1_ref[0, :, pl.ds(l0, LANES)]
        s2 = s2_ref[0, :, pl.ds(l0, LANES)]
        v1, rank1 = _top_ranked(s1, kk)
        v2, rank2 = _top_ranked(s2, kk)
        jrow8 = lax.broadcasted_iota(jnp.int32, (8, LANES), 0)
        blocks = [v1[0:1] + v2]
        for i in range(1, 8):
            blocks.append(v1[i:i + 1] + jnp.where(jrow8 < kk // (i + 1), v2[0:8], NEG_INF))
        blocks.append(v1[8:16] + v2[0:1])
        cand = jnp.concatenate(blocks, axis=0)
        top = cand[0:1]
        work = cand
        tau = top
        for _ in range(kk):
            tau = jnp.max(work, axis=0, keepdims=True)
            work = jnp.where(work == tau, NEG_INF, work)
        sel = cand >= tau
        zsum = jnp.sum(jnp.where(sel, jnp.exp(jnp.where(sel, cand - top, 0.0)), 0.0),
                       axis=0, keepdims=True)
        counts = [jnp.sum(jnp.where(blocks[i] >= tau, 1.0, 0.0), axis=0, keepdims=True)
                  for i in range(8)]
        counts.append(jnp.where(blocks[8] >= tau, 1.0, 0.0))
        nrank = jnp.concatenate(counts, axis=0)
        n1 = jnp.zeros(s1.shape, F32)
        for i in range(kk):
            n1 = jnp.where(rank1 == float(i), nrank[i:i + 1], n1)
        n1_ref[0, :, pl.ds(l0, LANES)] = n1
        p1_ref[0, :, pl.ds(l0, LANES)] = jnp.exp(s1 - v1[0:1]) / zsum
        r2_ref[:, pl.ds(l0, LANES)] = rank2.astype(BF16)
        p2_ref[:, pl.ds(l0, LANES)] = jnp.exp(s2 - v2[0:1]).astype(BF16)
        return 0

    lax.fori_loop(0, tl // LANES, sub, 0)


def _peer_topk(s1t, s2t, tl):
    hh, kk, n = s1t.shape
    spec = pl.BlockSpec((1, kk, tl), lambda i, h: (h, 0, i))
    f32 = jax.ShapeDtypeStruct((hh, kk, n), F32)
    b16 = jax.ShapeDtypeStruct((hh * kk, n), BF16)
    spec2 = pl.BlockSpec((kk, tl), lambda i, h: (h, i))
    return pl.pallas_call(
        functools.partial(_topk_kernel, tl=tl),
        out_shape=(f32, f32, b16, b16),
        grid=(n // tl, hh),
        in_specs=[spec, spec],
        out_specs=(spec, spec, spec2, spec2),
        compiler_params=pltpu.CompilerParams(
            dimension_semantics=("parallel", "parallel"), vmem_limit_bytes=VMEM_LIMIT),
        name="peer_topk",
    )(s1t, s2t)


def _dense_kernel(h2t_ref, u_ref, vt_ref, r2_ref, p2_ref, n1_ref, p1_ref, x1_ref, o_ref,
                  acc_scr, w_scr, r2_scr, p2_scr, *, ec):
    j = pl.program_id(1)
    nk = PEER_N_KEYS
    tl = h2t_ref.shape[1]
    pk = 16

    @pl.when(j == 0)
    def _():
        acc_scr[...] = jnp.zeros_like(acc_scr)
        r2_scr[...] = r2_ref[...]
        p2_scr[...] = p2_ref[...]

    pre = _dot(u_ref[...], h2t_ref[...])
    for e in range(ec // nk):
        e1 = j * (ec // nk) + e
        n1_rows = [jnp.broadcast_to(n1_ref[hh, pl.ds(e1, 1), :], (pk, tl)).astype(BF16)
                   for hh in range(PEER_HEADS)]
        p1_rows = [jnp.broadcast_to(p1_ref[hh, pl.ds(e1, 1), :], (pk, tl)).astype(BF16)
                   for hh in range(PEER_HEADS)]
        for tv in range(tl // LANES):
            lanes = slice(tv * LANES, (tv + 1) * LANES)
            g = None
            for hh in range(PEER_HEADS):
                keys = slice(hh * nk, (hh + 1) * nk)
                n1 = jnp.concatenate([n1_rows[hh][:, lanes]] * (nk // pk), axis=0)
                p1 = jnp.concatenate([p1_rows[hh][:, lanes]] * (nk // pk), axis=0)
                p2 = p2_scr[keys, lanes]
                term = jnp.where(r2_scr[keys, lanes] < n1, p2, jnp.zeros_like(p2)) * p1
                g = term if g is None else g + term
            blk = pre[e * nk:(e + 1) * nk, lanes]
            act = 0.5 * blk * (1.0 + lax.erf(blk * (2.0 ** -0.5)))
            w_scr[pl.ds(e * nk, nk), lanes] = act.astype(BF16) * g
    acc_scr[...] += _dot(vt_ref[...], w_scr[...])

    @pl.when(j == pl.num_programs(1) - 1)
    def _():
        o_ref[...] = x1_ref[...] + acc_scr[...].T


def _peer_dense(h2t, u_tab, vt_tab, r2, p2, n1, p1, x1, tl, ec):
    d, n = h2t.shape
    ne = u_tab.shape[0]
    hspec = pl.BlockSpec((PEER_HEADS, PEER_N_KEYS, tl), lambda i, j: (0, 0, i))
    bspec = pl.BlockSpec((PEER_HEADS * PEER_N_KEYS, tl), lambda i, j: (0, i))
    return pl.pallas_call(
        functools.partial(_dense_kernel, ec=ec),
        out_shape=jax.ShapeDtypeStruct((n, d), F32),
        grid=(n // tl, ne // ec),
        in_specs=[
            pl.BlockSpec((d, tl), lambda i, j: (0, i)),
            pl.BlockSpec((ec, d), lambda i, j: (j, 0)),
            pl.BlockSpec((d, ec), lambda i, j: (0, j)),
            bspec, bspec, hspec, hspec,
            pl.BlockSpec((tl, d), lambda i, j: (i, 0)),
        ],
        out_specs=pl.BlockSpec((tl, d), lambda i, j: (i, 0)),
        scratch_shapes=[pltpu.VMEM((d, tl), F32), pltpu.VMEM((ec, tl), BF16),
                        pltpu.VMEM((PEER_HEADS * PEER_N_KEYS, tl), BF16),
                        pltpu.VMEM((PEER_HEADS * PEER_N_KEYS, tl), BF16)],
        compiler_params=pltpu.CompilerParams(
            dimension_semantics=("parallel", "arbitrary"), vmem_limit_bytes=VMEM_LIMIT),
        name="peer_dense",
    )(h2t, u_tab, vt_tab, r2, p2, n1, p1, x1)


def _tile(n, pref):
    t = min(n, pref)
    assert n % t == 0, (n, t)
    return t


def kernel(x, mix_norm_w, w_in, sb_q_norm_w, sb_k_norm_w, gdn_conv_w, gdn_a_log, gdn_dt_bias,
           gdn_out_norm_w, w_branch_sb, w_branch_gdn, w_out, ffn_norm_w, peer_w_q, peer_keys1,
           peer_keys2, peer_u, peer_v):
    b, s, d = x.shape
    n = b * s
    depth = w_in.shape[0]
    sb_w = HEADS * HEAD_DIM
    small0 = 7 * sb_w
    small1 = small0 + 2 * HEADS
    nchunk = s // GDN_CHUNK
    x2d = x.reshape(n, d)
    for l in range(depth):
        w_main = jnp.concatenate([w_in[l][:, :small0], w_in[l][:, small1:]], axis=1).astype(BF16)
        w_small = jnp.pad(w_in[l][:, small0:small1], ((0, 0), (0, LANES - 2 * HEADS))).astype(BF16)
        proj_main, proj_small = _in_proj(x2d, mix_norm_w[l][None], w_main, w_small,
                                         _tile(n, 1024), 1024)
        proj3 = proj_main.reshape(b, s, MAIN_WIDTH)

        o_sb = _sb_attention(proj3, sb_q_norm_w[l][None], sb_k_norm_w[l][None], _tile(s, 256))

        ba = proj_small[:, :2 * HEADS].reshape(b, nchunk, GDN_CHUNK, 2 * HEADS)
        ba = ba.transpose(0, 3, 1, 2)
        b_col = ba[:, :HEADS, :, :, None]
        a_col = ba[:, HEADS:, :, :, None]
        a_row = ba[:, HEADS:, :, None, :]
        o_gdn = _gdn(proj3, gdn_conv_w[l], gdn_a_log[l], gdn_dt_bias[l], a_col, b_col, a_row,
                     gdn_out_norm_w[l][None])

        x1, h2, s1t, s2t = _merge(
            o_sb.reshape(n, sb_w), o_gdn.reshape(n, sb_w), proj_main, x2d,
            w_branch_sb[l].astype(BF16), w_branch_gdn[l].astype(BF16), w_out[l].astype(BF16),
            ffn_norm_w[l][None], peer_w_q[l].astype(BF16),
            peer_keys1[l].astype(BF16), peer_keys2[l].astype(BF16), _tile(n, 256))

        n1, p1, r2, p2 = _peer_topk(s1t, s2t, _tile(n, 1024))
        x2d = _peer_dense(h2.T, peer_u[l].astype(BF16), peer_v[l].astype(BF16).T,
                          r2, p2, n1, p1, x1, _tile(n, 512), 2048)
    return x2d.reshape(b, s, d)
```

```python
import functools

import jax
import jax.numpy as jnp
import numpy as np
from jax import lax
from jax.experimental import pallas as pl
from jax.experimental.pallas import tpu as pltpu

F32 = jnp.float32
BF16 = jnp.bfloat16

D_MODEL = 1024
HEADS = 8
HEAD_DIM = 128
LANES = 128
GDN_CONV = 4
GDN_CHUNK = 64
PEER_HEADS = 8
PEER_N_KEYS = 128
PEER_HALF = 128
PEER_TOPK = 16
EPS = 1e-6
MAIN_WIDTH = 9 * D_MODEL
VMEM_LIMIT = 56 * 1024 * 1024
SB_LOG_UNDERFLOW = -104.0
SB_GROUP = 4
GDN_GROUP = 2
GDN_UNROLL = 8
NEG_INF = float("-inf")
POS_INF = float("inf")


def _sigmoid(x):
    return 1.0 / (1.0 + jnp.exp(-x))


def _softplus(x):
    return jnp.maximum(x, 0.0) + jnp.log1p(jnp.exp(-jnp.abs(x)))


def _rms(x, w):
    return x * lax.rsqrt(jnp.mean(x * x, axis=-1, keepdims=True) + EPS) * w


def _dot(a, b):
    return jnp.dot(a, b, preferred_element_type=F32)


def _dot_nt(a, b):
    return lax.dot_general(a, b, (((1,), (1,)), ((), ())), preferred_element_type=F32)


def _dot_tn(a, b):
    return lax.dot_general(a, b, (((0,), (0,)), ((), ())), preferred_element_type=F32)


def _dot_hi(a, b):
    return jnp.dot(a, b, preferred_element_type=F32, precision=lax.Precision.HIGHEST)


def _inproj_kernel(x_ref, nw_ref, w_ref, ws_ref, o_ref, os_ref, h_scr):
    j = pl.program_id(1)

    @pl.when(j == 0)
    def _():
        h = _rms(x_ref[...], nw_ref[...]).astype(BF16)
        h_scr[...] = h
        os_ref[...] = _dot(h, ws_ref[...])

    o_ref[...] = _dot(h_scr[...], w_ref[...]).astype(o_ref.dtype)


def _in_proj(x2d, norm_w, w_main, w_small, tm, tn):
    n = x2d.shape[0]
    return pl.pallas_call(
        _inproj_kernel,
        out_shape=(jax.ShapeDtypeStruct((n, MAIN_WIDTH), BF16),
                   jax.ShapeDtypeStruct((n, LANES), F32)),
        grid=(n // tm, MAIN_WIDTH // tn),
        in_specs=[
            pl.BlockSpec((tm, D_MODEL), lambda i, j: (i, 0)),
            pl.BlockSpec((1, D_MODEL), lambda i, j: (0, 0)),
            pl.BlockSpec((D_MODEL, tn), lambda i, j: (0, j)),
            pl.BlockSpec((D_MODEL, LANES), lambda i, j: (0, 0)),
        ],
        out_specs=(pl.BlockSpec((tm, tn), lambda i, j: (i, j)),
                   pl.BlockSpec((tm, LANES), lambda i, j: (i, 0))),
        scratch_shapes=[pltpu.VMEM((tm, D_MODEL), BF16)],
        compiler_params=pltpu.CompilerParams(
            dimension_semantics=("parallel", "arbitrary"), vmem_limit_bytes=VMEM_LIMIT),
        name="in_proj",
    )(x2d, norm_w, w_main, w_small)


def _sb_kernel(q_ref, k_ref, v_ref, qw_ref, kw_ref, tri_ref, o_ref, kn_scr, *, tq):
    i = pl.program_id(2)
    kb = LANES
    dh = HEAD_DIM
    heads = range(SB_GROUP)

    @pl.when(i == 0)
    def _():
        for g in heads:
            kn_scr[:, g * dh:(g + 1) * dh] = _rms(
                k_ref[0, :, g * dh:(g + 1) * dh].astype(F32), kw_ref[...]).astype(BF16)

    scale = dh ** -0.5
    qn = [(_rms(q_ref[0, :, g * dh:(g + 1) * dh].astype(F32), qw_ref[...]) * scale).astype(BF16)
          for g in heads]
    row = i * tq + lax.broadcasted_iota(jnp.int32, (tq, kb), 0)
    col0 = lax.broadcasted_iota(jnp.int32, (tq, kb), 1)
    npair = (i + 1) * (tq // (2 * kb))

    def cond(carry):
        p, cs, _ = carry
        cmax = cs[0]
        for g in heads[1:]:
            cmax = jnp.maximum(cmax, cs[g])
        return jnp.logical_and(p < npair, jnp.max(cmax) > SB_LOG_UNDERFLOW)

    def body(carry):
        p, cs, accs = carry
        k0 = pl.multiple_of((npair - 1 - p) * (2 * kb), 2 * kb)
        masks = [(col0 + (k0 + half * kb)) < row for half in range(2)]
        z2 = [_dot_nt(qn[g], kn_scr[pl.ds(k0, 2 * kb), g * dh:(g + 1) * dh]) for g in heads]
        cats, logits = [], []
        for g in heads:
            cat_g, logit_g = [], []
            for half in range(2):
                z = z2[g][:, half * kb:(half + 1) * kb]
                sp = jnp.maximum(z, 0.0) + jnp.log(1.0 + jnp.exp(-jnp.abs(z)))
                lneg = jnp.where(masks[half], -sp, 0.0)
                hi = lneg.astype(BF16)
                lo = (lneg - hi.astype(F32)).astype(BF16)
                cat_g.append(jnp.concatenate([hi, lo], axis=1))
                logit_g.append(z - sp)
            cats.append(jnp.concatenate(cat_g, axis=0))
            logits.append(logit_g)
        res = [_dot(cats[g], tri_ref[...]) for g in heads]
        a2, c_new = [], []
        for g in heads:
            c_early = cs[g] + res[g][tq:, kb:]
            a_late = jnp.where(masks[1], jnp.exp(logits[g][1] + res[g][tq:, :kb] + cs[g]), 0.0)
            a_early = jnp.where(masks[0], jnp.exp(logits[g][0] + res[g][:tq, :kb] + c_early), 0.0)
            a2.append(jnp.concatenate([a_early, a_late], axis=1).astype(BF16))
            c_new.append(c_early + res[g][:tq, kb:])
        accs = tuple(accs[g] + _dot(a2[g], v_ref[0, pl.ds(k0, 2 * kb), g * dh:(g + 1) * dh])
                     for g in heads)
        return p + 1, tuple(c_new), accs

    init = (jnp.int32(0), tuple(jnp.zeros((tq, kb), F32) for _ in heads),
            tuple(jnp.zeros((tq, dh), F32) for _ in heads))
    _, _, accs = lax.while_loop(cond, body, init)
    for g in heads:
        o_ref[0, :, g * dh:(g + 1) * dh] = accs[g].astype(o_ref.dtype)


def _sb_tri():
    r = np.arange(2 * LANES)[:, None] % LANES
    c = np.arange(2 * LANES)[None, :]
    m = np.where(c < LANES, r > c, True)
    return jnp.asarray(m, dtype=BF16)


def _sb_attention(proj3, qw, kw, tq):
    b, s, _ = proj3.shape
    w = SB_GROUP * HEAD_DIM
    ng = HEADS // SB_GROUP
    return pl.pallas_call(
        functools.partial(_sb_kernel, tq=tq),
        out_shape=jax.ShapeDtypeStruct((b, s, HEADS * HEAD_DIM), BF16),
        grid=(b, ng, s // tq),
        in_specs=[
            pl.BlockSpec((1, tq, w), lambda bi, h, i: (bi, i, h)),
            pl.BlockSpec((1, s, w), lambda bi, h, i: (bi, 0, ng + h)),
            pl.BlockSpec((1, s, w), lambda bi, h, i: (bi, 0, 2 * ng + h)),
            pl.BlockSpec((1, HEAD_DIM), lambda bi, h, i: (0, 0)),
            pl.BlockSpec((1, HEAD_DIM), lambda bi, h, i: (0, 0)),
            pl.BlockSpec((2 * LANES, 2 * LANES), lambda bi, h, i: (0, 0)),
        ],
        out_specs=pl.BlockSpec((1, tq, w), lambda bi, h, i: (bi, i, h)),
        scratch_shapes=[pltpu.VMEM((s, w), BF16)],
        compiler_params=pltpu.CompilerParams(
            dimension_semantics=("parallel", "parallel", "arbitrary"),
            vmem_limit_bytes=VMEM_LIMIT),
        name="sb_attn",
    )(proj3, proj3, proj3, qw, kw, _sb_tri())


def _gdn_kernel(alog_ref, dtb_ref, gq_ref, gk_ref, gv_ref, z_ref, cwq_ref, cwk_ref, cwv_ref,
                acol_ref, bcol_ref, arow_ref, onw_ref, o_ref,
                xpad, qs, ks, vs, m_scr, b_scr, qp_scr, op_scr, eg_scr):
    hg = pl.program_id(1)
    s = qs.shape[0]
    c = GDN_CHUNK
    dk = HEAD_DIM
    w = GDN_GROUP * dk
    pad = 8

    def conv_silu(src_ref, cw_ref):
        xpad[pl.ds(0, pad), :] = jnp.zeros((pad, w), F32)
        xpad[pl.ds(pad, s), :] = src_ref[0].astype(F32)
        y = cw_ref[0:1, :] * xpad[pl.ds(pad - 3, s), :]
        for t in range(1, GDN_CONV):
            y = y + cw_ref[t:t + 1, :] * xpad[pl.ds(pad - 3 + t, s), :]
        return y * _sigmoid(y)

    def l2n(x):
        return x * lax.rsqrt(jnp.sum(x * x, axis=-1, keepdims=True) + EPS)

    yq = conv_silu(gq_ref, cwq_ref)
    for g in range(GDN_GROUP):
        qs[:, g * dk:(g + 1) * dk] = l2n(yq[:, g * dk:(g + 1) * dk]) * (dk ** -0.5)
    yk = conv_silu(gk_ref, cwk_ref)
    for g in range(GDN_GROUP):
        ks[:, g * dk:(g + 1) * dk] = l2n(yk[:, g * dk:(g + 1) * dk])
    vs[...] = conv_silu(gv_ref, cwv_ref)

    a_gain = [jnp.exp(jnp.full((1, 1), alog_ref[hg * GDN_GROUP + g], F32)) for g in range(GDN_GROUP)]
    dtb = [jnp.full((1, 1), dtb_ref[hg * GDN_GROUP + g], F32) for g in range(GDN_GROUP)]

    ri = lax.broadcasted_iota(jnp.int32, (c, c), 0)
    ci = lax.broadcasted_iota(jnp.int32, (c, c), 1)
    tril = ri >= ci
    strict = ri > ci
    tril_b = jnp.where(tril, 1.0, 0.0).astype(BF16)
    triu_b = jnp.where(ri <= ci, 1.0, 0.0).astype(BF16)
    eye = jnp.where(ri == ci, 1.0, 0.0)

    def split(x):
        hi = x.astype(BF16)
        return hi, (x - hi.astype(F32)).astype(BF16)

    def prepare(nn, carry):
        pairs = [(g, nn * GDN_UNROLL + j) for j in range(GDN_UNROLL) for g in range(GDN_GROUP)]
        ps = range(len(pairs))
        rows = [pl.ds(pl.multiple_of(n * c, c), c) for _, n in pairs]
        q = [qs[rows[i], g * dk:(g + 1) * dk] for i, (g, _) in enumerate(pairs)]
        k = [ks[rows[i], g * dk:(g + 1) * dk] for i, (g, _) in enumerate(pairs)]
        v = [vs[rows[i], g * dk:(g + 1) * dk] for i, (g, _) in enumerate(pairs)]
        g_col = [-a_gain[g] * _softplus(acol_ref[0, g, n] + dtb[g]) for g, n in pairs]
        g_row = [-a_gain[g] * _softplus(arow_ref[0, g, n] + dtb[g]) for g, n in pairs]
        beta = [jnp.broadcast_to(_sigmoid(bcol_ref[0, g, n]), (c, dk)) for g, n in pairs]
        gcs = [split(jnp.broadcast_to(g_col[i], (c, dk))) for i in ps]
        gc = [_dot(tril_b, gcs[i][0]) + _dot(tril_b, gcs[i][1]) for i in ps]
        grs = [split(jnp.broadcast_to(g_row[i], (8, c))) for i in ps]
        gc_row = [(_dot(grs[i][0], triu_b) + _dot(grs[i][1], triu_b))[0:1] for i in ps]
        decay = [jnp.where(tril, jnp.exp(jnp.where(tril, gc[i][:, :c] - gc_row[i], 0.0)), 0.0)
                 for i in ps]
        kbf = [k[i].astype(BF16) for i in ps]
        k_beta = [k[i] * beta[i] for i in ps]
        u = [-jnp.where(strict, _dot_nt(k_beta[i].astype(BF16), kbf[i]) * decay[i], 0.0) for i in ps]
        mb = [u[i].astype(BF16) for i in ps]
        for _ in range(5):
            m2 = [_dot(mb[i], mb[i]) for i in ps]
            mb = [m2[i].astype(BF16) for i in ps]
            u = [u[i] + m2[i] + _dot(u[i].astype(BF16), mb[i]) for i in ps]
        tb = [(eye + u[i]).astype(BF16) for i in ps]
        egc = [jnp.exp(gc[i]) for i in ps]
        vc_b = [_dot(tb[i], (v[i] * beta[i]).astype(BF16)).astype(BF16) for i in ps]
        kcd_b = [_dot(tb[i], (k_beta[i] * egc[i]).astype(BF16)).astype(BF16) for i in ps]
        attn_b = [jnp.where(tril, _dot_nt(q[i].astype(BF16), kbf[i]) * decay[i], 0.0).astype(BF16)
                  for i in ps]
        g_last = [gc[i][c - 1:c, :] for i in ps]
        kw_b = [(k[i] * jnp.exp(g_last[i] - gc[i])).astype(BF16) for i in ps]
        m_out = [_dot_tn(kw_b[i], kcd_b[i]).astype(BF16) for i in ps]
        b_out = [_dot_tn(kw_b[i], vc_b[i]) for i in ps]
        qp_out = [(q[i] * egc[i] - _dot(attn_b[i], kcd_b[i])).astype(BF16) for i in ps]
        op_out = [_dot(attn_b[i], vc_b[i]) for i in ps]
        for i, (g, n) in enumerate(pairs):
            m_scr[g, n] = m_out[i]
            b_scr[g, n] = b_out[i]
            qp_scr[g, n] = qp_out[i]
            op_scr[g, n] = op_out[i]
            eg_scr[g, n] = jnp.broadcast_to(jnp.exp(g_last[i]), (8, dk))
        return carry

    lax.fori_loop(0, s // (c * GDN_UNROLL), prepare, 0)

    def scan(n, states):
        r0 = pl.multiple_of(n * c, c)
        gs = range(GDN_GROUP)
        sb = [states[g].astype(BF16) for g in gs]
        zc = [z_ref[0, pl.ds(r0, c), g * dk:(g + 1) * dk].astype(F32) for g in gs]
        o = [_dot(qp_scr[g, n], sb[g]) + op_scr[g, n] for g in gs]
        new = tuple(states[g] * eg_scr[g, n][0:1] - _dot(m_scr[g, n], sb[g]) + b_scr[g, n]
                    for g in gs)
        for g in gs:
            o_ref[0, pl.ds(r0, c), g * dk:(g + 1) * dk] = (
                _rms(o[g], onw_ref[...]) * (zc[g] * _sigmoid(zc[g]))).astype(o_ref.dtype)
        return new

    lax.fori_loop(0, s // c, scan, tuple(jnp.zeros((dk, dk), F32) for _ in range(GDN_GROUP)))


def _gdn(proj3, conv_w, a_log, dt_bias, a_col, b_col, a_row, out_norm_w):
    b, s, _ = proj3.shape
    n = s // GDN_CHUNK
    c = GDN_CHUNK
    gg = GDN_GROUP
    w = gg * HEAD_DIM
    col = lambda base: pl.BlockSpec((1, s, w), lambda bi, h: (bi, 0, base // gg + h))
    cw = lambda base: pl.BlockSpec((GDN_CONV, w), lambda bi, h: (0, base // gg + h))
    smem = pl.BlockSpec(memory_space=pltpu.SMEM)
    return pl.pallas_call(
        _gdn_kernel,
        out_shape=jax.ShapeDtypeStruct((b, s, HEADS * HEAD_DIM), BF16),
        grid=(b, HEADS // gg),
        in_specs=[
            smem, smem,
            col(3 * HEADS), col(4 * HEADS), col(5 * HEADS), col(6 * HEADS),
            cw(0), cw(HEADS), cw(2 * HEADS),
            pl.BlockSpec((1, gg, n, c, 1), lambda bi, h: (bi, h, 0, 0, 0)),
            pl.BlockSpec((1, gg, n, c, 1), lambda bi, h: (bi, h, 0, 0, 0)),
            pl.BlockSpec((1, gg, n, 1, c), lambda bi, h: (bi, h, 0, 0, 0)),
            pl.BlockSpec((1, HEAD_DIM), lambda bi, h: (0, 0)),
        ],
        out_specs=pl.BlockSpec((1, s, w), lambda bi, h: (bi, 0, h)),
        scratch_shapes=[
            pltpu.VMEM((s + 8, w), F32), pltpu.VMEM((s, w), F32), pltpu.VMEM((s, w), F32),
            pltpu.VMEM((s, w), F32),
            pltpu.VMEM((gg, n, HEAD_DIM, HEAD_DIM), BF16), pltpu.VMEM((gg, n, HEAD_DIM, HEAD_DIM), F32),
            pltpu.VMEM((gg, n, c, HEAD_DIM), BF16), pltpu.VMEM((gg, n, c, HEAD_DIM), F32),
            pltpu.VMEM((gg, n, 8, HEAD_DIM), F32),
        ],
        compiler_params=pltpu.CompilerParams(
            dimension_semantics=("parallel", "parallel"), vmem_limit_bytes=VMEM_LIMIT),
        name="gdn",
    )(a_log, dt_bias, proj3, proj3, proj3, proj3, conv_w, conv_w, conv_w,
      a_col, b_col, a_row, out_norm_w)


def _merge_kernel(osb_ref, ogdn_ref, gsb_ref, ggdn_ref, x_ref, wsb_ref, wgdn_ref, wout_ref,
                  fnw_ref, wq_ref, k1_ref, k2_ref, x1_ref, h2_ref, s1_ref, s2_ref):
    y_sb = _dot(osb_ref[...], wsb_ref[...])
    y_gdn = _dot(ogdn_ref[...], wgdn_ref[...])
    merged = (_sigmoid(gsb_ref[...].astype(F32)) * y_sb
              + _sigmoid(ggdn_ref[...].astype(F32)) * y_gdn)
    x1 = x_ref[...] + _dot(merged.astype(BF16), wout_ref[...])
    x1_ref[...] = x1
    h2 = _rms(x1, fnw_ref[...]).astype(BF16)
    h2_ref[...] = h2
    q = _dot(h2, wq_ref[...]).astype(BF16)
    for hh in range(PEER_HEADS):
        base = hh * 2 * PEER_HALF
        s1_ref[hh] = _dot_nt(k1_ref[hh], q[:, base:base + PEER_HALF])
        s2_ref[hh] = _dot_nt(k2_ref[hh], q[:, base + PEER_HALF:base + 2 * PEER_HALF])


def _merge(o_sb, o_gdn, proj_main, x2d, w_sb, w_gdn, w_out, ffn_w, w_q, keys1, keys2, tm):
    n = x2d.shape[0]
    full = lambda shape: pl.BlockSpec(shape, lambda i: (0,) * len(shape))
    tok = lambda width, cb=0: pl.BlockSpec((tm, width), lambda i, cb=cb: (i, cb))
    st_spec = pl.BlockSpec((PEER_HEADS, PEER_N_KEYS, tm), lambda i: (0, 0, i))
    return pl.pallas_call(
        _merge_kernel,
        out_shape=(jax.ShapeDtypeStruct((n, D_MODEL), F32),
                   jax.ShapeDtypeStruct((n, D_MODEL), BF16),
                   jax.ShapeDtypeStruct((PEER_HEADS, PEER_N_KEYS, n), F32),
                   jax.ShapeDtypeStruct((PEER_HEADS, PEER_N_KEYS, n), F32)),
        grid=(n // tm,),
        in_specs=[
            tok(D_MODEL), tok(D_MODEL), tok(D_MODEL, 7), tok(D_MODEL, 8), tok(D_MODEL),
            full((D_MODEL, D_MODEL)), full((D_MODEL, D_MODEL)), full((D_MODEL, D_MODEL)),
            full((1, D_MODEL)), full((D_MODEL, 2 * PEER_HALF * PEER_HEADS)),
            full((PEER_HEADS, PEER_N_KEYS, PEER_HALF)), full((PEER_HEADS, PEER_N_KEYS, PEER_HALF)),
        ],
        out_specs=(tok(D_MODEL), tok(D_MODEL), st_spec, st_spec),
        compiler_params=pltpu.CompilerParams(
            dimension_semantics=("parallel",), vmem_limit_bytes=VMEM_LIMIT),
        name="merge",
    )(o_sb, o_gdn, proj_main, proj_main, x2d, w_sb, w_gdn, w_out, ffn_w, w_q, keys1, keys2)


def _top_ranked(s, k):
    assert k <= 16
    big = 2.0 ** 100
    s = jnp.maximum(s, -0.5 * big)
    vals = []
    for r in range(k):
        m = jnp.max(s, axis=0, keepdims=True)
        vals.append(m)
        s = jnp.where(s == m, -big * (1.0 + r / 16.0), s)
    rank = jnp.where(s <= -big, (s * (-1.0 / big) - 1.0) * 16.0, float(k))
    return jnp.concatenate(vals, axis=0), rank


def _topk_kernel(s1_ref, s2_ref, n1_ref, p1_ref, r2_ref, p2_ref, *, tl):
    kk = PEER_TOPK

    def sub(bi, _):
        l0 = pl.multiple_of(bi * LANES, LANES)
        s1 = s1_ref[0, :, pl.ds(l0, LANES)]
        s2 = s2_ref[0, :, pl.ds(l0, LANES)]
        v1, rank1 = _top_ranked(s1, kk)
        v2, rank2 = _top_ranked(s2, kk)
        jrow8 = lax.broadcasted_iota(jnp.int32, (8, LANES), 0)
        blocks = [v1[0:1] + v2]
        for i in range(1, 8):
            blocks.append(v1[i:i + 1] + jnp.where(jrow8 < kk // (i + 1), v2[0:8], NEG_INF))
        blocks.append(v1[8:16] + v2[0:1])
        cand = jnp.concatenate(blocks, axis=0)
        top = cand[0:1]
        work = cand
        tau = top
        for _ in range(kk):
            tau = jnp.max(work, axis=0, keepdims=True)
            work = jnp.where(work == tau, NEG_INF, work)
        sel = cand >= tau
        zsum = jnp.sum(jnp.where(sel, jnp.exp(jnp.where(sel, cand - top, 0.0)), 0.0),
                       axis=0, keepdims=True)
        counts = [jnp.sum(jnp.where(blocks[i] >= tau, 1.0, 0.0), axis=0, keepdims=True)
                  for i in range(8)]
        counts.append(jnp.where(blocks[8] >= tau, 1.0, 0.0))
        nrank = jnp.concatenate(counts, axis=0)
        n1 = jnp.zeros(s1.shape, F32)
        for i in range(kk):
            n1 = jnp.where(rank1 == float(i), nrank[i:i + 1], n1)
        n1_ref[0, :, pl.ds(l0, LANES)] = n1
        p1_ref[0, :, pl.ds(l0, LANES)] = jnp.exp(s1 - v1[0:1]) / zsum
        r2_ref[:, pl.ds(l0, LANES)] = rank2.astype(BF16)
        p2_ref[:, pl.ds(l0, LANES)] = jnp.exp(s2 - v2[0:1]).astype(BF16)
        return 0

    lax.fori_loop(0, tl // LANES, sub, 0)


def _peer_topk(s1t, s2t, tl):
    hh, kk, n = s1t.shape
    spec = pl.BlockSpec((1, kk, tl), lambda i, h: (h, 0, i))
    f32 = jax.ShapeDtypeStruct((hh, kk, n), F32)
    b16 = jax.ShapeDtypeStruct((hh * kk, n), BF16)
    spec2 = pl.BlockSpec((kk, tl), lambda i, h: (h, i))
    return pl.pallas_call(
        functools.partial(_topk_kernel, tl=tl),
        out_shape=(f32, f32, b16, b16),
        grid=(n // tl, hh),
        in_specs=[spec, spec],
        out_specs=(spec, spec, spec2, spec2),
        compiler_params=pltpu.CompilerParams(
            dimension_semantics=("parallel", "parallel"), vmem_limit_bytes=VMEM_LIMIT),
        name="peer_topk",
    )(s1t, s2t)


def _dense_kernel(h2t_ref, u_ref, vt_ref, r2_ref, p2_ref, n1_ref, p1_ref, x1_ref, o_ref,
                  acc_scr, w_scr, r2_scr, p2_scr, *, ec):
    j = pl.program_id(1)
    nk = PEER_N_KEYS
    tl = h2t_ref.shape[1]
    pk = 16

    @pl.when(j == 0)
    def _():
        acc_scr[...] = jnp.zeros_like(acc_scr)
        r2_scr[...] = r2_ref[...]
        p2_scr[...] = p2_ref[...]

    pre = _dot(u_ref[...], h2t_ref[...])
    for e in range(ec // nk):
        e1 = j * (ec // nk) + e
        n1_rows = [jnp.broadcast_to(n1_ref[hh, pl.ds(e1, 1), :], (pk, tl)).astype(BF16)
                   for hh in range(PEER_HEADS)]
        p1_rows = [jnp.broadcast_to(p1_ref[hh, pl.ds(e1, 1), :], (pk, tl)).astype(BF16)
                   for hh in range(PEER_HEADS)]
        for tv in range(tl // LANES):
            lanes = slice(tv * LANES, (tv + 1) * LANES)
            g = None
            for hh in range(PEER_HEADS):
                keys = slice(hh * nk, (hh + 1) * nk)
                n1 = jnp.concatenate([n1_rows[hh][:, lanes]] * (nk // pk), axis=0)
                p1 = jnp.concatenate([p1_rows[hh][:, lanes]] * (nk // pk), axis=0)
                p2 = p2_scr[keys, lanes]
                term = jnp.where(r2_scr[keys, lanes] < n1, p2, jnp.zeros_like(p2)) * p1
                g = term if g is None else g + term
            blk = pre[e * nk:(e + 1) * nk, lanes]
            act = 0.5 * blk * (1.0 + lax.erf(blk * (2.0 ** -0.5)))
            w_scr[pl.ds(e * nk, nk), lanes] = act.astype(BF16) * g
    acc_scr[...] += _dot(vt_ref[...], w_scr[...])

    @pl.when(j == pl.num_programs(1) - 1)
    def _():
        o_ref[...] = x1_ref[...] + acc_scr[...].T


def _peer_dense(h2t, u_tab, vt_tab, r2, p2, n1, p1, x1, tl, ec):
    d, n = h2t.shape
    ne = u_tab.shape[0]
    hspec = pl.BlockSpec((PEER_HEADS, PEER_N_KEYS, tl), lambda i, j: (0, 0, i))
    bspec = pl.BlockSpec((PEER_HEADS * PEER_N_KEYS, tl), lambda i, j: (0, i))
    return pl.pallas_call(
        functools.partial(_dense_kernel, ec=ec),
        out_shape=jax.ShapeDtypeStruct((n, d), F32),
        grid=(n // tl, ne // ec),
        in_specs=[
            pl.BlockSpec((d, tl), lambda i, j: (0, i)),
            pl.BlockSpec((ec, d), lambda i, j: (j, 0)),
            pl.BlockSpec((d, ec), lambda i, j: (0, j)),
            bspec, bspec, hspec, hspec,
            pl.BlockSpec((tl, d), lambda i, j: (i, 0)),
        ],
        out_specs=pl.BlockSpec((tl, d), lambda i, j: (i, 0)),
        scratch_shapes=[pltpu.VMEM((d, tl), F32), pltpu.VMEM((ec, tl), BF16),
                        pltpu.VMEM((PEER_HEADS * PEER_N_KEYS, tl), BF16),
                        pltpu.VMEM((PEER_HEADS * PEER_N_KEYS, tl), BF16)],
        compiler_params=pltpu.CompilerParams(
            dimension_semantics=("parallel", "arbitrary"), vmem_limit_bytes=VMEM_LIMIT),
        name="peer_dense",
    )(h2t, u_tab, vt_tab, r2, p2, n1, p1, x1)


def _tile(n, pref):
    t = min(n, pref)
    assert n % t == 0, (n, t)
    return t


def kernel(x, mix_norm_w, w_in, sb_q_norm_w, sb_k_norm_w, gdn_conv_w, gdn_a_log, gdn_dt_bias,
           gdn_out_norm_w, w_branch_sb, w_branch_gdn, w_out, ffn_norm_w, peer_w_q, peer_keys1,
           peer_keys2, peer_u, peer_v):
    b, s, d = x.shape
    n = b * s
    depth = w_in.shape[0]
    sb_w = HEADS * HEAD_DIM
    small0 = 7 * sb_w
    small1 = small0 + 2 * HEADS
    nchunk = s // GDN_CHUNK
    x2d = x.reshape(n, d)
    for l in range(depth):
        w_main = jnp.concatenate([w_in[l][:, :small0], w_in[l][:, small1:]], axis=1).astype(BF16)
        w_small = jnp.pad(w_in[l][:, small0:small1], ((0, 0), (0, LANES - 2 * HEADS))).astype(BF16)
        proj_main, proj_small = _in_proj(x2d, mix_norm_w[l][None], w_main, w_small,
                                         _tile(n, 1024), 1024)
        proj3 = proj_main.reshape(b, s, MAIN_WIDTH)

        o_sb = _sb_attention(proj3, sb_q_norm_w[l][None], sb_k_norm_w[l][None], _tile(s, 256))

        ba = proj_small[:, :2 * HEADS].reshape(b, nchunk, GDN_CHUNK, 2 * HEADS)
        ba = ba.transpose(0, 3, 1, 2)
        b_col = ba[:, :HEADS, :, :, None]
        a_col = ba[:, HEADS:, :, :, None]
        a_row = ba[:, HEADS:, :, None, :]
        o_gdn = _gdn(proj3, gdn_conv_w[l], gdn_a_log[l], gdn_dt_bias[l], a_col, b_col, a_row,
                     gdn_out_norm_w[l][None])

        x1, h2, s1t, s2t = _merge(
            o_sb.reshape(n, sb_w), o_gdn.reshape(n, sb_w), proj_main, x2d,
            w_branch_sb[l].astype(BF16), w_branch_gdn[l].astype(BF16), w_out[l].astype(BF16),
            ffn_norm_w[l][None], peer_w_q[l].astype(BF16),
            peer_keys1[l].astype(BF16), peer_keys2[l].astype(BF16), _tile(n, 256))

        n1, p1, r2, p2 = _peer_topk(s1t, s2t, _tile(n, 1024))
        x2d = _peer_dense(h2.T, peer_u[l].astype(BF16), peer_v[l].astype(BF16).T,
                          r2, p2, n1, p1, x1, _tile(n, 512), 2048)
    return x2d.reshape(b, s, d)
```

```python
import functools

import jax
import jax.numpy as jnp
import numpy as np
from jax import lax
from jax.experimental import pallas as pl
from jax.experimental.pallas import tpu as pltpu

F32 = jnp.float32
BF16 = jnp.bfloat16

D_MODEL = 1024
HEADS = 8
HEAD_DIM = 128
LANES = 128
GDN_CONV = 4
GDN_CHUNK = 64
PEER_HEADS = 8
PEER_N_KEYS = 128
PEER_HALF = 128
PEER_TOPK = 16
EPS = 1e-6
MAIN_WIDTH = 9 * D_MODEL
VMEM_LIMIT = 56 * 1024 * 1024
SB_LOG_UNDERFLOW = -104.0
SB_GROUP = 4
GDN_GROUP = 2
GDN_UNROLL = 8
NEG_INF = float("-inf")
POS_INF = float("inf")


def _sigmoid(x):
    return 1.0 / (1.0 + jnp.exp(-x))


def _softplus(x):
    return jnp.maximum(x, 0.0) + jnp.log1p(jnp.exp(-jnp.abs(x)))


def _rms(x, w):
    return x * lax.rsqrt(jnp.mean(x * x, axis=-1, keepdims=True) + EPS) * w


def _dot(a, b):
    return jnp.dot(a, b, preferred_element_type=F32)


def _dot_nt(a, b):
    return lax.dot_general(a, b, (((1,), (1,)), ((), ())), preferred_element_type=F32)


def _dot_tn(a, b):
    return lax.dot_general(a, b, (((0,), (0,)), ((), ())), preferred_element_type=F32)


def _dot_hi(a, b):
    return jnp.dot(a, b, preferred_element_type=F32, precision=lax.Precision.HIGHEST)


def _inproj_kernel(x_ref, nw_ref, w_ref, ws_ref, o_ref, os_ref, h_scr):
    j = pl.program_id(1)

    @pl.when(j == 0)
    def _():
        h = _rms(x_ref[...], nw_ref[...]).astype(BF16)
        h_scr[...] = h
        os_ref[...] = _dot(h, ws_ref[...])

    o_ref[...] = _dot(h_scr[...], w_ref[...]).astype(o_ref.dtype)


def _in_proj(x2d, norm_w, w_main, w_small, tm, tn):
    n = x2d.shape[0]
    return pl.pallas_call(
        _inproj_kernel,
        out_shape=(jax.ShapeDtypeStruct((n, MAIN_WIDTH), BF16),
                   jax.ShapeDtypeStruct((n, LANES), F32)),
        grid=(n // tm, MAIN_WIDTH // tn),
        in_specs=[
            pl.BlockSpec((tm, D_MODEL), lambda i, j: (i, 0)),
            pl.BlockSpec((1, D_MODEL), lambda i, j: (0, 0)),
            pl.BlockSpec((D_MODEL, tn), lambda i, j: (0, j)),
            pl.BlockSpec((D_MODEL, LANES), lambda i, j: (0, 0)),
        ],
        out_specs=(pl.BlockSpec((tm, tn), lambda i, j: (i, j)),
                   pl.BlockSpec((tm, LANES), lambda i, j: (i, 0))),
        scratch_shapes=[pltpu.VMEM((tm, D_MODEL), BF16)],
        compiler_params=pltpu.CompilerParams(
            dimension_semantics=("parallel", "arbitrary"), vmem_limit_bytes=VMEM_LIMIT),
        name="in_proj",
    )(x2d, norm_w, w_main, w_small)


def _sb_kernel(q_ref, k_ref, v_ref, qw_ref, kw_ref, tri_ref, o_ref, kn_scr, *, tq):
    i = pl.program_id(2)
    kb = LANES
    dh = HEAD_DIM
    heads = range(SB_GROUP)

    @pl.when(i == 0)
    def _():
        for g in heads:
            kn_scr[:, g * dh:(g + 1) * dh] = _rms(
                k_ref[0, :, g * dh:(g + 1) * dh].astype(F32), kw_ref[...]).astype(BF16)

    scale = dh ** -0.5
    qn = [(_rms(q_ref[0, :, g * dh:(g + 1) * dh].astype(F32), qw_ref[...]) * scale).astype(BF16)
          for g in heads]
    row = i * tq + lax.broadcasted_iota(jnp.int32, (tq, kb), 0)
    col0 = lax.broadcasted_iota(jnp.int32, (tq, kb), 1)
    npair = (i + 1) * (tq // (2 * kb))

    def cond(carry):
        p, cs, _ = carry
        cmax = cs[0]
        for g in heads[1:]:
            cmax = jnp.maximum(cmax, cs[g])
        return jnp.logical_and(p < npair, jnp.max(cmax) > SB_LOG_UNDERFLOW)

    def body(carry):
        p, cs, accs = carry
        k0 = pl.multiple_of((npair - 1 - p) * (2 * kb), 2 * kb)
        masks = [(col0 + (k0 + half * kb)) < row for half in range(2)]
        z2 = [_dot_nt(qn[g], kn_scr[pl.ds(k0, 2 * kb), g * dh:(g + 1) * dh]) for g in heads]
        cats, logits = [], []
        for g in heads:
            cat_g, logit_g = [], []
            for half in range(2):
                z = z2[g][:, half * kb:(half + 1) * kb]
                sp = jnp.maximum(z, 0.0) + jnp.log(1.0 + jnp.exp(-jnp.abs(z)))
                lneg = jnp.where(masks[half], -sp, 0.0)
                hi = lneg.astype(BF16)
                lo = (lneg - hi.astype(F32)).astype(BF16)
                cat_g.append(jnp.concatenate([hi, lo], axis=1))
                logit_g.append(z - sp)
            cats.append(jnp.concatenate(cat_g, axis=0))
            logits.append(logit_g)
        res = [_dot(cats[g], tri_ref[...]) for g in heads]
        a2, c_new = [], []
        for g in heads:
            c_early = cs[g] + res[g][tq:, kb:]
            a_late = jnp.where(masks[1], jnp.exp(logits[g][1] + res[g][tq:, :kb] + cs[g]), 0.0)
            a_early = jnp.where(masks[0], jnp.exp(logits[g][0] + res[g][:tq, :kb] + c_early), 0.0)
            a2.append(jnp.concatenate([a_early, a_late], axis=1).astype(BF16))
            c_new.append(c_early + res[g][:tq, kb:])
        accs = tuple(accs[g] + _dot(a2[g], v_ref[0, pl.ds(k0, 2 * kb), g * dh:(g + 1) * dh])
                     for g in heads)
        return p + 1, tuple(c_new), accs

    init = (jnp.int32(0), tuple(jnp.zeros((tq, kb), F32) for _ in heads),
            tuple(jnp.zeros((tq, dh), F32) for _ in heads))
    _, _, accs = lax.while_loop(cond, body, init)
    for g in heads:
        o_ref[0, :, g * dh:(g + 1) * dh] = accs[g].astype(o_ref.dtype)


def _sb_tri():
    r = np.arange(2 * LANES)[:, None] % LANES
    c = np.arange(2 * LANES)[None, :]
    m = np.where(c < LANES, r > c, True)
    return jnp.asarray(m, dtype=BF16)


def _sb_attention(proj3, qw, kw, tq):
    b, s, _ = proj3.shape
    w = SB_GROUP * HEAD_DIM
    ng = HEADS // SB_GROUP
    return pl.pallas_call(
        functools.partial(_sb_kernel, tq=tq),
        out_shape=jax.ShapeDtypeStruct((b, s, HEADS * HEAD_DIM), BF16),
        grid=(b, ng, s // tq),
        in_specs=[
            pl.BlockSpec((1, tq, w), lambda bi, h, i: (bi, i, h)),
            pl.BlockSpec((1, s, w), lambda bi, h, i: (bi, 0, ng + h)),
            pl.BlockSpec((1, s, w), lambda bi, h, i: (bi, 0, 2 * ng + h)),
            pl.BlockSpec((1, HEAD_DIM), lambda bi, h, i: (0, 0)),
            pl.BlockSpec((1, HEAD_DIM), lambda bi, h, i: (0, 0)),
            pl.BlockSpec((2 * LANES, 2 * LANES), lambda bi, h, i: (0, 0)),
        ],
        out_specs=pl.BlockSpec((1, tq, w), lambda bi, h, i: (bi, i, h)),
        scratch_shapes=[pltpu.VMEM((s, w), BF16)],
        compiler_params=pltpu.CompilerParams(
            dimension_semantics=("parallel", "parallel", "arbitrary"),
            vmem_limit_bytes=VMEM_LIMIT),
        name="sb_attn",
    )(proj3, proj3, proj3, qw, kw, _sb_tri())


def _gdn_kernel(alog_ref, dtb_ref, gq_ref, gk_ref, gv_ref, z_ref, cwq_ref, cwk_ref, cwv_ref,
                acol_ref, bcol_ref, arow_ref, onw_ref, o_ref,
                xpad, qs, ks, vs, m_scr, b_scr, qp_scr, op_scr, eg_scr):
    hg = pl.program_id(1)
    s = qs.shape[0]
    c = GDN_CHUNK
    dk = HEAD_DIM
    w = GDN_GROUP * dk
    pad = 8

    def conv_silu(src_ref, cw_ref):
        xpad[pl.ds(0, pad), :] = jnp.zeros((pad, w), F32)
        xpad[pl.ds(pad, s), :] = src_ref[0].astype(F32)
        y = cw_ref[0:1, :] * xpad[pl.ds(pad - 3, s), :]
        for t in range(1, GDN_CONV):
            y = y + cw_ref[t:t + 1, :] * xpad[pl.ds(pad - 3 + t, s), :]
        return y * _sigmoid(y)

    def l2n(x):
        return x * lax.rsqrt(jnp.sum(x * x, axis=-1, keepdims=True) + EPS)

    yq = conv_silu(gq_ref, cwq_ref)
    for g in range(GDN_GROUP):
        qs[:, g * dk:(g + 1) * dk] = l2n(yq[:, g * dk:(g + 1) * dk]) * (dk ** -0.5)
    yk = conv_silu(gk_ref, cwk_ref)
    for g in range(GDN_GROUP):
        ks[:, g * dk:(g + 1) * dk] = l2n(yk[:, g * dk:(g + 1) * dk])
    vs[...] = conv_silu(gv_ref, cwv_ref)

    a_gain = [jnp.exp(jnp.full((1, 1), alog_ref[hg * GDN_GROUP + g], F32)) for g in range(GDN_GROUP)]
    dtb = [jnp.full((1, 1), dtb_ref[hg * GDN_GROUP + g], F32) for g in range(GDN_GROUP)]

    ri = lax.broadcasted_iota(jnp.int32, (c, c), 0)
    ci = lax.broadcasted_iota(jnp.int32, (c, c), 1)
    tril = ri >= ci
    strict = ri > ci
    tril_b = jnp.where(tril, 1.0, 0.0).astype(BF16)
    triu_b = jnp.where(ri <= ci, 1.0, 0.0).astype(BF16)
    eye = jnp.where(ri == ci, 1.0, 0.0)

    def split(x):
        hi = x.astype(BF16)
        return hi, (x - hi.astype(F32)).astype(BF16)

    def prepare(nn, carry):
        pairs = [(g, nn * GDN_UNROLL + j) for j in range(GDN_UNROLL) for g in range(GDN_GROUP)]
        ps = range(len(pairs))
        rows = [pl.ds(pl.multiple_of(n * c, c), c) for _, n in pairs]
        q = [qs[rows[i], g * dk:(g + 1) * dk] for i, (g, _) in enumerate(pairs)]
        k = [ks[rows[i], g * dk:(g + 1) * dk] for i, (g, _) in enumerate(pairs)]
        v = [vs[rows[i], g * dk:(g + 1) * dk] for i, (g, _) in enumerate(pairs)]
        g_col = [-a_gain[g] * _softplus(acol_ref[0, g, n] + dtb[g]) for g, n in pairs]
        g_row = [-a_gain[g] * _softplus(arow_ref[0, g, n] + dtb[g]) for g, n in pairs]
        beta = [jnp.broadcast_to(_sigmoid(bcol_ref[0, g, n]), (c, dk)) for g, n in pairs]
        gcs = [split(jnp.broadcast_to(g_col[i], (c, dk))) for i in ps]
        gc = [_dot(tril_b, gcs[i][0]) + _dot(tril_b, gcs[i][1]) for i in ps]
        grs = [split(jnp.broadcast_to(g_row[i], (8, c))) for i in ps]
        gc_row = [(_dot(grs[i][0], triu_b) + _dot(grs[i][1], triu_b))[0:1] for i in ps]
        decay = [jnp.where(tril, jnp.exp(jnp.where(tril, gc[i][:, :c] - gc_row[i], 0.0)), 0.0)
                 for i in ps]
        kbf = [k[i].astype(BF16) for i in ps]
        k_beta = [k[i] * beta[i] for i in ps]
        u = [-jnp.where(strict, _dot_nt(k_beta[i].astype(BF16), kbf[i]) * decay[i], 0.0) for i in ps]
        mb = [u[i].astype(BF16) for i in ps]
        for _ in range(5):
            m2 = [_dot(mb[i], mb[i]) for i in ps]
            mb = [m2[i].astype(BF16) for i in ps]
            u = [u[i] + m2[i] + _dot(u[i].astype(BF16), mb[i]) for i in ps]
        tb = [(eye + u[i]).astype(BF16) for i in ps]
        egc = [jnp.exp(gc[i]) for i in ps]
        vc_b = [_dot(tb[i], (v[i] * beta[i]).astype(BF16)).astype(BF16) for i in ps]
        kcd_b = [_dot(tb[i], (k_beta[i] * egc[i]).astype(BF16)).astype(BF16) for i in ps]
        attn_b = [jnp.where(tril, _dot_nt(q[i].astype(BF16), kbf[i]) * decay[i], 0.0).astype(BF16)
                  for i in ps]
        g_last = [gc[i][c - 1:c, :] for i in ps]
        kw_b = [(k[i] * jnp.exp(g_last[i] - gc[i])).astype(BF16) for i in ps]
        m_out = [_dot_tn(kw_b[i], kcd_b[i]).astype(BF16) for i in ps]
        b_out = [_dot_tn(kw_b[i], vc_b[i]) for i in ps]
        qp_out = [(q[i] * egc[i] - _dot(attn_b[i], kcd_b[i])).astype(BF16) for i in ps]
        op_out = [_dot(attn_b[i], vc_b[i]) for i in ps]
        for i, (g, n) in enumerate(pairs):
            m_scr[g, n] = m_out[i]
            b_scr[g, n] = b_out[i]
            qp_scr[g, n] = qp_out[i]
            op_scr[g, n] = op_out[i]
            eg_scr[g, n] = jnp.broadcast_to(jnp.exp(g_last[i]), (8, dk))
        return carry

    lax.fori_loop(0, s // (c * GDN_UNROLL), prepare, 0)

    def scan(n, states):
        r0 = pl.multiple_of(n * c, c)
        gs = range(GDN_GROUP)
        sb = [states[g].astype(BF16) for g in gs]
        zc = [z_ref[0, pl.ds(r0, c), g * dk:(g + 1) * dk].astype(F32) for g in gs]
        o = [_dot(qp_scr[g, n], sb[g]) + op_scr[g, n] for g in gs]
        new = tuple(states[g] * eg_scr[g, n][0:1] - _dot(m_scr[g, n], sb[g]) + b_scr[g, n]
                    for g in gs)
        for g in gs:
            o_ref[0, pl.ds(r0, c), g * dk:(g + 1) * dk] = (
                _rms(o[g], onw_ref[...]) * (zc[g] * _sigmoid(zc[g]))).astype(o_ref.dtype)
        return new

    lax.fori_loop(0, s // c, scan, tuple(jnp.zeros((dk, dk), F32) for _ in range(GDN_GROUP)))


def _gdn(proj3, conv_w, a_log, dt_bias, a_col, b_col, a_row, out_norm_w):
    b, s, _ = proj3.shape
    n = s // GDN_CHUNK
    c = GDN_CHUNK
    gg = GDN_GROUP
    w = gg * HEAD_DIM
    col = lambda base: pl.BlockSpec((1, s, w), lambda bi, h: (bi, 0, base // gg + h))
    cw = lambda base: pl.BlockSpec((GDN_CONV, w), lambda bi, h: (0, base // gg + h))
    smem = pl.BlockSpec(memory_space=pltpu.SMEM)
    return pl.pallas_call(
        _gdn_kernel,
        out_shape=jax.ShapeDtypeStruct((b, s, HEADS * HEAD_DIM), BF16),
        grid=(b, HEADS // gg),
        in_specs=[
            smem, smem,
            col(3 * HEADS), col(4 * HEADS), col(5 * HEADS), col(6 * HEADS),
            cw(0), cw(HEADS), cw(2 * HEADS),
            pl.BlockSpec((1, gg, n, c, 1), lambda bi, h: (bi, h, 0, 0, 0)),
            pl.BlockSpec((1, gg, n, c, 1), lambda bi, h: (bi, h, 0, 0, 0)),
            pl.BlockSpec((1, gg, n, 1, c), lambda bi, h: (bi, h, 0, 0, 0)),
            pl.BlockSpec((1, HEAD_DIM), lambda bi, h: (0, 0)),
        ],
        out_specs=pl.BlockSpec((1, s, w), lambda bi, h: (bi, 0, h)),
        scratch_shapes=[
            pltpu.VMEM((s + 8, w), F32), pltpu.VMEM((s, w), F32), pltpu.VMEM((s, w), F32),
            pltpu.VMEM((s, w), F32),
            pltpu.VMEM((gg, n, HEAD_DIM, HEAD_DIM), BF16), pltpu.VMEM((gg, n, HEAD_DIM, HEAD_DIM), F32),
            pltpu.VMEM((gg, n, c, HEAD_DIM), BF16), pltpu.VMEM((gg, n, c, HEAD_DIM), F32),
            pltpu.VMEM((gg, n, 8, HEAD_DIM), F32),
        ],
        compiler_params=pltpu.CompilerParams(
            dimension_semantics=("parallel", "parallel"), vmem_limit_bytes=VMEM_LIMIT),
        name="gdn",
    )(a_log, dt_bias, proj3, proj3, proj3, proj3, conv_w, conv_w, conv_w,
      a_col, b_col, a_row, out_norm_w)


def _merge_kernel(osb_ref, ogdn_ref, gsb_ref, ggdn_ref, x_ref, wsb_ref, wgdn_ref, wout_ref,
                  fnw_ref, wq_ref, k1_ref, k2_ref, x1_ref, h2_ref, s1_ref, s2_ref):
    y_sb = _dot(osb_ref[...], wsb_ref[...])
    y_gdn = _dot(ogdn_ref[...], wgdn_ref[...])
    merged = (_sigmoid(gsb_ref[...].astype(F32)) * y_sb
              + _sigmoid(ggdn_ref[...].astype(F32)) * y_gdn)
    x1 = x_ref[...] + _dot(merged.astype(BF16), wout_ref[...])
    x1_ref[...] = x1
    h2 = _rms(x1, fnw_ref[...]).astype(BF16)
    h2_ref[...] = h2
    q = _dot(h2, wq_ref[...]).astype(BF16)
    for hh in range(PEER_HEADS):
        base = hh * 2 * PEER_HALF
        s1_ref[hh] = _dot_nt(k1_ref[hh], q[:, base:base + PEER_HALF])
        s2_ref[hh] = _dot_nt(k2_ref[hh], q[:, base + PEER_HALF:base + 2 * PEER_HALF])


def _merge(o_sb, o_gdn, proj_main, x2d, w_sb, w_gdn, w_out, ffn_w, w_q, keys1, keys2, tm):
    n = x2d.shape[0]
    full = lambda shape: pl.BlockSpec(shape, lambda i: (0,) * len(shape))
    tok = lambda width, cb=0: pl.BlockSpec((tm, width), lambda i, cb=cb: (i, cb))
    st_spec = pl.BlockSpec((PEER_HEADS, PEER_N_KEYS, tm), lambda i: (0, 0, i))
    return pl.pallas_call(
        _merge_kernel,
        out_shape=(jax.ShapeDtypeStruct((n, D_MODEL), F32),
                   jax.ShapeDtypeStruct((n, D_MODEL), BF16),
                   jax.ShapeDtypeStruct((PEER_HEADS, PEER_N_KEYS, n), F32),
                   jax.ShapeDtypeStruct((PEER_HEADS, PEER_N_KEYS, n), F32)),
        grid=(n // tm,),
        in_specs=[
            tok(D_MODEL), tok(D_MODEL), tok(D_MODEL, 7), tok(D_MODEL, 8), tok(D_MODEL),
            full((D_MODEL, D_MODEL)), full((D_MODEL, D_MODEL)), full((D_MODEL, D_MODEL)),
            full((1, D_MODEL)), full((D_MODEL, 2 * PEER_HALF * PEER_HEADS)),
            full((PEER_HEADS, PEER_N_KEYS, PEER_HALF)), full((PEER_HEADS, PEER_N_KEYS, PEER_HALF)),
        ],
        out_specs=(tok(D_MODEL), tok(D_MODEL), st_spec, st_spec),
        compiler_params=pltpu.CompilerParams(
            dimension_semantics=("parallel",), vmem_limit_bytes=VMEM_LIMIT),
        name="merge",
    )(o_sb, o_gdn, proj_main, proj_main, x2d, w_sb, w_gdn, w_out, ffn_w, w_q, keys1, keys2)


def _top_ranked(s, k):
    assert k <= 16
    big = 2.0 ** 100
    s = jnp.maximum(s, -0.5 * big)
    vals = []
    for r in range(k):
        m = jnp.max(s, axis=0, keepdims=True)
        vals.append(m)
        s = jnp.where(s == m, -big * (1.0 + r / 16.0), s)
    rank = jnp.where(s <= -big, (s * (-1.0 / big) - 1.0) * 16.0, float(k))
    return jnp.concatenate(vals, axis=0), rank


def _topk_kernel(s1_ref, s2_ref, n1_ref, p1_ref, r2_ref, p2_ref, *, tl):
    kk = PEER_TOPK

    def sub(bi, _):
        l0 = pl.multiple_of(bi * LANES, LANES)
        s1 = s1_ref[0, :, pl.ds(l0, LANES)]
        s2 = s2_ref[0, :, pl.ds(l0, LANES)]
        v1, rank1 = _top_ranked(s1, kk)
        v2, rank2 = _top_ranked(s2, kk)
        jrow8 = lax.broadcasted_iota(jnp.int32, (8, LANES), 0)
        blocks = [v1[0:1] + v2]
        for i in range(1, 8):
            blocks.append(v1[i:i + 1] + jnp.where(jrow8 < kk // (i + 1), v2[0:8], NEG_INF))
        blocks.append(v1[8:16] + v2[0:1])
        cand = jnp.concatenate(blocks, axis=0)
        top = cand[0:1]
        work = cand
        tau = top
        for _ in range(kk):
            tau = jnp.max(work, axis=0, keepdims=True)
            work = jnp.where(work == tau, NEG_INF, work)
        sel = cand >= tau
        zsum = jnp.sum(jnp.where(sel, jnp.exp(jnp.where(sel, cand - top, 0.0)), 0.0),
                       axis=0, keepdims=True)
        counts = [jnp.sum(jnp.where(blocks[i] >= tau, 1.0, 0.0), axis=0, keepdims=True)
                  for i in range(8)]
        counts.append(jnp.where(blocks[8] >= tau, 1.0, 0.0))
        nrank = jnp.concatenate(counts, axis=0)
        n1 = jnp.zeros(s1.shape, F32)
        for i in range(kk):
            n1 = jnp.where(rank1 == float(i), nrank[i:i + 1], n1)
        n1_ref[0, :, pl.ds(l0, LANES)] = n1
        p1_ref[0, :, pl.ds(l0, LANES)] = jnp.exp(s1 - v1[0:1]) / zsum
        r2_ref[:, pl.ds(l0, LANES)] = rank2.astype(BF16)
        p2_ref[:, pl.ds(l0, LANES)] = jnp.exp(s2 - v2[0:1]).astype(BF16)
        return 0

    lax.fori_loop(0, tl // LANES, sub, 0)


def _peer_topk(s1t, s2t, tl):
    hh, kk, n = s1t.shape
    spec = pl.BlockSpec((1, kk, tl), lambda i, h: (h, 0, i))
    f32 = jax.ShapeDtypeStruct((hh, kk, n), F32)
    b16 = jax.ShapeDtypeStruct((hh * kk, n), BF16)
    spec2 = pl.BlockSpec((kk, tl), lambda i, h: (h, i))
    return pl.pallas_call(
        functools.partial(_topk_kernel, tl=tl),
        out_shape=(f32, f32, b16, b16),
        grid=(n // tl, hh),
        in_specs=[spec, spec],
        out_specs=(spec, spec, spec2, spec2),
        compiler_params=pltpu.CompilerParams(
            dimension_semantics=("parallel", "parallel"), vmem_limit_bytes=VMEM_LIMIT),
        name="peer_topk",
    )(s1t, s2t)


def _dense_kernel(u0_ref, h2n_ref, un_ref, vt_ref, r2_ref, p2_ref, n1_ref, p1_ref, x1_ref, o_ref,
                  acc_scr, w_scr, r2_scr, p2_scr, pre_a, pre_b, *, ec):
    i = pl.program_id(0)
    j = pl.program_id(1)
    nk = PEER_N_KEYS
    tl = h2n_ref.shape[1]
    pk = 16

    @pl.when(jnp.logical_and(i == 0, j == 0))
    def _():
        pre_a[...] = _dot(u0_ref[...], h2n_ref[...])

    @pl.when(j == 0)
    def _():
        acc_scr[...] = jnp.zeros_like(acc_scr)
        r2_scr[...] = r2_ref[...]
        p2_scr[...] = p2_ref[...]

    def step(pre_now, pre_next):
        pre_next[...] = _dot(un_ref[...], h2n_ref[...])
        for e in range(ec // nk):
            e1 = j * (ec // nk) + e
            n1_rows = [jnp.broadcast_to(n1_ref[hh, pl.ds(e1, 1), :], (pk, tl)).astype(BF16)
                       for hh in range(PEER_HEADS)]
            p1_rows = [jnp.broadcast_to(p1_ref[hh, pl.ds(e1, 1), :], (pk, tl)).astype(BF16)
                       for hh in range(PEER_HEADS)]
            for tv in range(tl // LANES):
                lanes = slice(tv * LANES, (tv + 1) * LANES)
                g = None
                for hh in range(PEER_HEADS):
                    keys = slice(hh * nk, (hh + 1) * nk)
                    n1 = jnp.concatenate([n1_rows[hh][:, lanes]] * (nk // pk), axis=0)
                    p1 = jnp.concatenate([p1_rows[hh][:, lanes]] * (nk // pk), axis=0)
                    p2 = p2_scr[keys, lanes]
                    term = jnp.where(r2_scr[keys, lanes] < n1, p2, jnp.zeros_like(p2)) * p1
                    g = term if g is None else g + term
                blk = pre_now[e * nk:(e + 1) * nk, lanes].astype(BF16)
                act = (0.5 * blk) * (1.0 + lax.erf(blk * (2.0 ** -0.5)))
                w_scr[pl.ds(e * nk, nk), lanes] = act * g
        acc_scr[...] += _dot(vt_ref[...], w_scr[...])

    @pl.when(j % 2 == 0)
    def _():
        step(pre_a, pre_b)

    @pl.when(j % 2 == 1)
    def _():
        step(pre_b, pre_a)

    @pl.when(j == pl.num_programs(1) - 1)
    def _():
        o_ref[...] = x1_ref[...] + acc_scr[...].T


def _peer_dense(h2t, u_tab, vt_tab, r2, p2, n1, p1, x1, tl, ec):
    d, n = h2t.shape
    ne = u_tab.shape[0]
    ni, nj = n // tl, ne // ec
    assert nj % 2 == 0
    hspec = pl.BlockSpec((PEER_HEADS, PEER_N_KEYS, tl), lambda i, j: (0, 0, i))
    bspec = pl.BlockSpec((PEER_HEADS * PEER_N_KEYS, tl), lambda i, j: (0, i))
    return pl.pallas_call(
        functools.partial(_dense_kernel, ec=ec),
        out_shape=jax.ShapeDtypeStruct((n, d), F32),
        grid=(ni, nj),
        in_specs=[
            pl.BlockSpec((ec, d), lambda i, j: (0, 0)),
            pl.BlockSpec((d, tl), lambda i, j: (0, jnp.minimum(i + (j + 1) // nj, ni - 1))),
            pl.BlockSpec((ec, d), lambda i, j: ((j + 1) % nj, 0)),
            pl.BlockSpec((d, ec), lambda i, j: (0, j)),
            bspec, bspec, hspec, hspec,
            pl.BlockSpec((tl, d), lambda i, j: (i, 0)),
        ],
        out_specs=pl.BlockSpec((tl, d), lambda i, j: (i, 0)),
        scratch_shapes=[pltpu.VMEM((d, tl), F32), pltpu.VMEM((ec, tl), BF16),
                        pltpu.VMEM((PEER_HEADS * PEER_N_KEYS, tl), BF16),
                        pltpu.VMEM((PEER_HEADS * PEER_N_KEYS, tl), BF16),
                        pltpu.VMEM((ec, tl), F32), pltpu.VMEM((ec, tl), F32)],
        compiler_params=pltpu.CompilerParams(
            dimension_semantics=("arbitrary", "arbitrary"), vmem_limit_bytes=VMEM_LIMIT),
        name="peer_dense",
    )(u_tab, h2t, u_tab, vt_tab, r2, p2, n1, p1, x1)


def _tile(n, pref):
    t = min(n, pref)
    assert n % t == 0, (n, t)
    return t


def kernel(x, mix_norm_w, w_in, sb_q_norm_w, sb_k_norm_w, gdn_conv_w, gdn_a_log, gdn_dt_bias,
           gdn_out_norm_w, w_branch_sb, w_branch_gdn, w_out, ffn_norm_w, peer_w_q, peer_keys1,
           peer_keys2, peer_u, peer_v):
    b, s, d = x.shape
    n = b * s
    depth = w_in.shape[0]
    sb_w = HEADS * HEAD_DIM
    small0 = 7 * sb_w
    small1 = small0 + 2 * HEADS
    nchunk = s // GDN_CHUNK
    x2d = x.reshape(n, d)
    for l in range(depth):
        w_main = jnp.concatenate([w_in[l][:, :small0], w_in[l][:, small1:]], axis=1).astype(BF16)
        w_small = jnp.pad(w_in[l][:, small0:small1], ((0, 0), (0, LANES - 2 * HEADS))).astype(BF16)
        proj_main, proj_small = _in_proj(x2d, mix_norm_w[l][None], w_main, w_small,
                                         _tile(n, 1024), 1024)
        proj3 = proj_main.reshape(b, s, MAIN_WIDTH)

        o_sb = _sb_attention(proj3, sb_q_norm_w[l][None], sb_k_norm_w[l][None], _tile(s, 256))

        ba = proj_small[:, :2 * HEADS].reshape(b, nchunk, GDN_CHUNK, 2 * HEADS)
        ba = ba.transpose(0, 3, 1, 2)
        b_col = ba[:, :HEADS, :, :, None]
        a_col = ba[:, HEADS:, :, :, None]
        a_row = ba[:, HEADS:, :, None, :]
        o_gdn = _gdn(proj3, gdn_conv_w[l], gdn_a_log[l], gdn_dt_bias[l], a_col, b_col, a_row,
                     gdn_out_norm_w[l][None])

        x1, h2, s1t, s2t = _merge(
            o_sb.reshape(n, sb_w), o_gdn.reshape(n, sb_w), proj_main, x2d,
            w_branch_sb[l].astype(BF16), w_branch_gdn[l].astype(BF16), w_out[l].astype(BF16),
            ffn_norm_w[l][None], peer_w_q[l].astype(BF16),
            peer_keys1[l].astype(BF16), peer_keys2[l].astype(BF16), _tile(n, 256))

        n1, p1, r2, p2 = _peer_topk(s1t, s2t, _tile(n, 1024))
        x2d = _peer_dense(h2.T, peer_u[l].astype(BF16), peer_v[l].astype(BF16).T,
                          r2, p2, n1, p1, x1, _tile(n, 512), 1024)
    return x2d.reshape(b, s, d)
```

```python
import functools

import jax
import jax.numpy as jnp
import numpy as np
from jax import lax
from jax.experimental import pallas as pl
from jax.experimental.pallas import tpu as pltpu

F32 = jnp.float32
BF16 = jnp.bfloat16

D_MODEL = 1024
HEADS = 8
HEAD_DIM = 128
LANES = 128
GDN_CONV = 4
GDN_CHUNK = 64
PEER_HEADS = 8
PEER_N_KEYS = 128
PEER_HALF = 128
PEER_TOPK = 16
EPS = 1e-6
MAIN_WIDTH = 9 * D_MODEL
VMEM_LIMIT = 56 * 1024 * 1024
SB_LOG_UNDERFLOW = -104.0
SB_GROUP = 4
GDN_GROUP = 2
GDN_UNROLL = 8
DENSE_SPLIT = 4
NEG_INF = float("-inf")
POS_INF = float("inf")


def _sigmoid(x):
    return 1.0 / (1.0 + jnp.exp(-x))


def _softplus(x):
    return jnp.maximum(x, 0.0) + jnp.log1p(jnp.exp(-jnp.abs(x)))


def _rms(x, w):
    return x * lax.rsqrt(jnp.mean(x * x, axis=-1, keepdims=True) + EPS) * w


def _dot(a, b):
    return jnp.dot(a, b, preferred_element_type=F32)


def _dot_nt(a, b):
    return lax.dot_general(a, b, (((1,), (1,)), ((), ())), preferred_element_type=F32)


def _dot_tn(a, b):
    return lax.dot_general(a, b, (((0,), (0,)), ((), ())), preferred_element_type=F32)


def _dot_hi(a, b):
    return jnp.dot(a, b, preferred_element_type=F32, precision=lax.Precision.HIGHEST)


def _inproj_kernel(x_ref, nw_ref, w_ref, ws_ref, o_ref, os_ref, h_scr):
    j = pl.program_id(1)

    @pl.when(j == 0)
    def _():
        h = _rms(x_ref[...], nw_ref[...]).astype(BF16)
        h_scr[...] = h
        os_ref[...] = _dot(h, ws_ref[...])

    o_ref[...] = _dot(h_scr[...], w_ref[...]).astype(o_ref.dtype)


def _in_proj(x2d, norm_w, w_main, w_small, tm, tn):
    n = x2d.shape[0]
    return pl.pallas_call(
        _inproj_kernel,
        out_shape=(jax.ShapeDtypeStruct((n, MAIN_WIDTH), BF16),
                   jax.ShapeDtypeStruct((n, LANES), F32)),
        grid=(n // tm, MAIN_WIDTH // tn),
        in_specs=[
            pl.BlockSpec((tm, D_MODEL), lambda i, j: (i, 0)),
            pl.BlockSpec((1, D_MODEL), lambda i, j: (0, 0)),
            pl.BlockSpec((D_MODEL, tn), lambda i, j: (0, j)),
            pl.BlockSpec((D_MODEL, LANES), lambda i, j: (0, 0)),
        ],
        out_specs=(pl.BlockSpec((tm, tn), lambda i, j: (i, j)),
                   pl.BlockSpec((tm, LANES), lambda i, j: (i, 0))),
        scratch_shapes=[pltpu.VMEM((tm, D_MODEL), BF16)],
        compiler_params=pltpu.CompilerParams(
            dimension_semantics=("parallel", "arbitrary"), vmem_limit_bytes=VMEM_LIMIT),
        name="in_proj",
    )(x2d, norm_w, w_main, w_small)


def _sb_kernel(q_ref, k_ref, v_ref, qw_ref, kw_ref, tri_ref, o_ref, kn_scr, *, tq):
    i = pl.program_id(2)
    kb = LANES
    dh = HEAD_DIM
    heads = range(SB_GROUP)

    @pl.when(i == 0)
    def _():
        for g in heads:
            kn_scr[:, g * dh:(g + 1) * dh] = _rms(
                k_ref[0, :, g * dh:(g + 1) * dh].astype(F32), kw_ref[...]).astype(BF16)

    scale = dh ** -0.5
    qn = [(_rms(q_ref[0, :, g * dh:(g + 1) * dh].astype(F32), qw_ref[...]) * scale).astype(BF16)
          for g in heads]
    row = i * tq + lax.broadcasted_iota(jnp.int32, (tq, kb), 0)
    col0 = lax.broadcasted_iota(jnp.int32, (tq, kb), 1)
    npair = (i + 1) * (tq // (2 * kb))

    def cond(carry):
        p, cs, _ = carry
        cmax = cs[0]
        for g in heads[1:]:
            cmax = jnp.maximum(cmax, cs[g])
        return jnp.logical_and(p < npair, jnp.max(cmax) > SB_LOG_UNDERFLOW)

    def body(carry):
        p, cs, accs = carry
        k0 = pl.multiple_of((npair - 1 - p) * (2 * kb), 2 * kb)
        masks = [(col0 + (k0 + half * kb)) < row for half in range(2)]
        z2 = [_dot_nt(qn[g], kn_scr[pl.ds(k0, 2 * kb), g * dh:(g + 1) * dh]) for g in heads]
        cats, logits = [], []
        for g in heads:
            cat_g, logit_g = [], []
            for half in range(2):
                z = z2[g][:, half * kb:(half + 1) * kb]
                sp = jnp.maximum(z, 0.0) + jnp.log(1.0 + jnp.exp(-jnp.abs(z)))
                lneg = jnp.where(masks[half], -sp, 0.0)
                hi = lneg.astype(BF16)
                lo = (lneg - hi.astype(F32)).astype(BF16)
                cat_g.append(jnp.concatenate([hi, lo], axis=1))
                logit_g.append(z - sp)
            cats.append(jnp.concatenate(cat_g, axis=0))
            logits.append(logit_g)
        res = [_dot(cats[g], tri_ref[...]) for g in heads]
        a2, c_new = [], []
        for g in heads:
            c_early = cs[g] + res[g][tq:, kb:]
            a_late = jnp.where(masks[1], jnp.exp(logits[g][1] + res[g][tq:, :kb] + cs[g]), 0.0)
            a_early = jnp.where(masks[0], jnp.exp(logits[g][0] + res[g][:tq, :kb] + c_early), 0.0)
            a2.append(jnp.concatenate([a_early, a_late], axis=1).astype(BF16))
            c_new.append(c_early + res[g][:tq, kb:])
        accs = tuple(accs[g] + _dot(a2[g], v_ref[0, pl.ds(k0, 2 * kb), g * dh:(g + 1) * dh])
                     for g in heads)
        return p + 1, tuple(c_new), accs

    init = (jnp.int32(0), tuple(jnp.zeros((tq, kb), F32) for _ in heads),
            tuple(jnp.zeros((tq, dh), F32) for _ in heads))
    _, _, accs = lax.while_loop(cond, body, init)
    for g in heads:
        o_ref[0, :, g * dh:(g + 1) * dh] = accs[g].astype(o_ref.dtype)


def _sb_tri():
    r = np.arange(2 * LANES)[:, None] % LANES
    c = np.arange(2 * LANES)[None, :]
    m = np.where(c < LANES, r > c, True)
    return jnp.asarray(m, dtype=BF16)


def _sb_attention(proj3, qw, kw, tq):
    b, s, _ = proj3.shape
    w = SB_GROUP * HEAD_DIM
    ng = HEADS // SB_GROUP
    return pl.pallas_call(
        functools.partial(_sb_kernel, tq=tq),
        out_shape=jax.ShapeDtypeStruct((b, s, HEADS * HEAD_DIM), BF16),
        grid=(b, ng, s // tq),
        in_specs=[
            pl.BlockSpec((1, tq, w), lambda bi, h, i: (bi, i, h)),
            pl.BlockSpec((1, s, w), lambda bi, h, i: (bi, 0, ng + h)),
            pl.BlockSpec((1, s, w), lambda bi, h, i: (bi, 0, 2 * ng + h)),
            pl.BlockSpec((1, HEAD_DIM), lambda bi, h, i: (0, 0)),
            pl.BlockSpec((1, HEAD_DIM), lambda bi, h, i: (0, 0)),
            pl.BlockSpec((2 * LANES, 2 * LANES), lambda bi, h, i: (0, 0)),
        ],
        out_specs=pl.BlockSpec((1, tq, w), lambda bi, h, i: (bi, i, h)),
        scratch_shapes=[pltpu.VMEM((s, w), BF16)],
        compiler_params=pltpu.CompilerParams(
            dimension_semantics=("parallel", "parallel", "arbitrary"),
            vmem_limit_bytes=VMEM_LIMIT),
        name="sb_attn",
    )(proj3, proj3, proj3, qw, kw, _sb_tri())


def _gdn_kernel(alog_ref, dtb_ref, gq_ref, gk_ref, gv_ref, z_ref, cwq_ref, cwk_ref, cwv_ref,
                acol_ref, bcol_ref, arow_ref, onw_ref, o_ref,
                xpad, qs, ks, vs, m_scr, b_scr, qp_scr, op_scr, eg_scr):
    hg = pl.program_id(1)
    s = qs.shape[0]
    c = GDN_CHUNK
    dk = HEAD_DIM
    w = GDN_GROUP * dk
    pad = 8

    def conv_silu(src_ref, cw_ref):
        xpad[pl.ds(0, pad), :] = jnp.zeros((pad, w), F32)
        xpad[pl.ds(pad, s), :] = src_ref[0].astype(F32)
        y = cw_ref[0:1, :] * xpad[pl.ds(pad - 3, s), :]
        for t in range(1, GDN_CONV):
            y = y + cw_ref[t:t + 1, :] * xpad[pl.ds(pad - 3 + t, s), :]
        return y * _sigmoid(y)

    def l2n(x):
        return x * lax.rsqrt(jnp.sum(x * x, axis=-1, keepdims=True) + EPS)

    yq = conv_silu(gq_ref, cwq_ref)
    for g in range(GDN_GROUP):
        qs[:, g * dk:(g + 1) * dk] = l2n(yq[:, g * dk:(g + 1) * dk]) * (dk ** -0.5)
    yk = conv_silu(gk_ref, cwk_ref)
    for g in range(GDN_GROUP):
        ks[:, g * dk:(g + 1) * dk] = l2n(yk[:, g * dk:(g + 1) * dk])
    vs[...] = conv_silu(gv_ref, cwv_ref)

    a_gain = [jnp.exp(jnp.full((1, 1), alog_ref[hg * GDN_GROUP + g], F32)) for g in range(GDN_GROUP)]
    dtb = [jnp.full((1, 1), dtb_ref[hg * GDN_GROUP + g], F32) for g in range(GDN_GROUP)]

    ri = lax.broadcasted_iota(jnp.int32, (c, c), 0)
    ci = lax.broadcasted_iota(jnp.int32, (c, c), 1)
    tril = ri >= ci
    strict = ri > ci
    tril_b = jnp.where(tril, 1.0, 0.0).astype(BF16)
    triu_b = jnp.where(ri <= ci, 1.0, 0.0).astype(BF16)
    eye = jnp.where(ri == ci, 1.0, 0.0)

    def split(x):
        hi = x.astype(BF16)
        return hi, (x - hi.astype(F32)).astype(BF16)

    def prepare(nn, carry):
        pairs = [(g, nn * GDN_UNROLL + j) for j in range(GDN_UNROLL) for g in range(GDN_GROUP)]
        ps = range(len(pairs))
        rows = [pl.ds(pl.multiple_of(n * c, c), c) for _, n in pairs]
        q = [qs[rows[i], g * dk:(g + 1) * dk] for i, (g, _) in enumerate(pairs)]
        k = [ks[rows[i], g * dk:(g + 1) * dk] for i, (g, _) in enumerate(pairs)]
        v = [vs[rows[i], g * dk:(g + 1) * dk] for i, (g, _) in enumerate(pairs)]
        g_col = [-a_gain[g] * _softplus(acol_ref[0, g, n] + dtb[g]) for g, n in pairs]
        g_row = [-a_gain[g] * _softplus(arow_ref[0, g, n] + dtb[g]) for g, n in pairs]
        beta = [jnp.broadcast_to(_sigmoid(bcol_ref[0, g, n]), (c, dk)) for g, n in pairs]
        gcs = [split(jnp.broadcast_to(g_col[i], (c, dk))) for i in ps]
        gc = [_dot(tril_b, gcs[i][0]) + _dot(tril_b, gcs[i][1]) for i in ps]
        grs = [split(jnp.broadcast_to(g_row[i], (8, c))) for i in ps]
        gc_row = [(_dot(grs[i][0], triu_b) + _dot(grs[i][1], triu_b))[0:1] for i in ps]
        decay = [jnp.where(tril, jnp.exp(jnp.where(tril, gc[i][:, :c] - gc_row[i], 0.0)), 0.0)
                 for i in ps]
        kbf = [k[i].astype(BF16) for i in ps]
        k_beta = [k[i] * beta[i] for i in ps]
        u = [-jnp.where(strict, _dot_nt(k_beta[i].astype(BF16), kbf[i]) * decay[i], 0.0) for i in ps]
        mb = [u[i].astype(BF16) for i in ps]
        for _ in range(5):
            m2 = [_dot(mb[i], mb[i]) for i in ps]
            mb = [m2[i].astype(BF16) for i in ps]
            u = [u[i] + m2[i] + _dot(u[i].astype(BF16), mb[i]) for i in ps]
        tb = [(eye + u[i]).astype(BF16) for i in ps]
        egc = [jnp.exp(gc[i]) for i in ps]
        vc_b = [_dot(tb[i], (v[i] * beta[i]).astype(BF16)).astype(BF16) for i in ps]
        kcd_b = [_dot(tb[i], (k_beta[i] * egc[i]).astype(BF16)).astype(BF16) for i in ps]
        attn_b = [jnp.where(tril, _dot_nt(q[i].astype(BF16), kbf[i]) * decay[i], 0.0).astype(BF16)
                  for i in ps]
        g_last = [gc[i][c - 1:c, :] for i in ps]
        kw_b = [(k[i] * jnp.exp(g_last[i] - gc[i])).astype(BF16) for i in ps]
        m_out = [_dot_tn(kw_b[i], kcd_b[i]).astype(BF16) for i in ps]
        b_out = [_dot_tn(kw_b[i], vc_b[i]) for i in ps]
        qp_out = [(q[i] * egc[i] - _dot(attn_b[i], kcd_b[i])).astype(BF16) for i in ps]
        op_out = [_dot(attn_b[i], vc_b[i]) for i in ps]
        for i, (g, n) in enumerate(pairs):
            m_scr[g, n] = m_out[i]
            b_scr[g, n] = b_out[i]
            qp_scr[g, n] = qp_out[i]
            op_scr[g, n] = op_out[i]
            eg_scr[g, n] = jnp.broadcast_to(jnp.exp(g_last[i]), (8, dk))
        return carry

    lax.fori_loop(0, s // (c * GDN_UNROLL), prepare, 0)

    def scan(n, states):
        r0 = pl.multiple_of(n * c, c)
        gs = range(GDN_GROUP)
        sb = [states[g].astype(BF16) for g in gs]
        zc = [z_ref[0, pl.ds(r0, c), g * dk:(g + 1) * dk].astype(F32) for g in gs]
        o = [_dot(qp_scr[g, n], sb[g]) + op_scr[g, n] for g in gs]
        new = tuple(states[g] * eg_scr[g, n][0:1] - _dot(m_scr[g, n], sb[g]) + b_scr[g, n]
                    for g in gs)
        for g in gs:
            o_ref[0, pl.ds(r0, c), g * dk:(g + 1) * dk] = (
                _rms(o[g], onw_ref[...]) * (zc[g] * _sigmoid(zc[g]))).astype(o_ref.dtype)
        return new

    lax.fori_loop(0, s // c, scan, tuple(jnp.zeros((dk, dk), F32) for _ in range(GDN_GROUP)))


def _gdn(proj3, conv_w, a_log, dt_bias, a_col, b_col, a_row, out_norm_w):
    b, s, _ = proj3.shape
    n = s // GDN_CHUNK
    c = GDN_CHUNK
    gg = GDN_GROUP
    w = gg * HEAD_DIM
    col = lambda base: pl.BlockSpec((1, s, w), lambda bi, h: (bi, 0, base // gg + h))
    cw = lambda base: pl.BlockSpec((GDN_CONV, w), lambda bi, h: (0, base // gg + h))
    smem = pl.BlockSpec(memory_space=pltpu.SMEM)
    return pl.pallas_call(
        _gdn_kernel,
        out_shape=jax.ShapeDtypeStruct((b, s, HEADS * HEAD_DIM), BF16),
        grid=(b, HEADS // gg),
        in_specs=[
            smem, smem,
            col(3 * HEADS), col(4 * HEADS), col(5 * HEADS), col(6 * HEADS),
            cw(0), cw(HEADS), cw(2 * HEADS),
            pl.BlockSpec((1, gg, n, c, 1), lambda bi, h: (bi, h, 0, 0, 0)),
            pl.BlockSpec((1, gg, n, c, 1), lambda bi, h: (bi, h, 0, 0, 0)),
            pl.BlockSpec((1, gg, n, 1, c), lambda bi, h: (bi, h, 0, 0, 0)),
            pl.BlockSpec((1, HEAD_DIM), lambda bi, h: (0, 0)),
        ],
        out_specs=pl.BlockSpec((1, s, w), lambda bi, h: (bi, 0, h)),
        scratch_shapes=[
            pltpu.VMEM((s + 8, w), F32), pltpu.VMEM((s, w), F32), pltpu.VMEM((s, w), F32),
            pltpu.VMEM((s, w), F32),
            pltpu.VMEM((gg, n, HEAD_DIM, HEAD_DIM), BF16), pltpu.VMEM((gg, n, HEAD_DIM, HEAD_DIM), F32),
            pltpu.VMEM((gg, n, c, HEAD_DIM), BF16), pltpu.VMEM((gg, n, c, HEAD_DIM), F32),
            pltpu.VMEM((gg, n, 8, HEAD_DIM), F32),
        ],
        compiler_params=pltpu.CompilerParams(
            dimension_semantics=("parallel", "parallel"), vmem_limit_bytes=VMEM_LIMIT),
        name="gdn",
    )(a_log, dt_bias, proj3, proj3, proj3, proj3, conv_w, conv_w, conv_w,
      a_col, b_col, a_row, out_norm_w)


def _merge_kernel(osb_ref, ogdn_ref, gsb_ref, ggdn_ref, x_ref, wsb_ref, wgdn_ref, wout_ref,
                  fnw_ref, wq_ref, k1_ref, k2_ref, x1_ref, h2_ref, s1_ref, s2_ref):
    y_sb = _dot(osb_ref[...], wsb_ref[...])
    y_gdn = _dot(ogdn_ref[...], wgdn_ref[...])
    merged = (_sigmoid(gsb_ref[...].astype(F32)) * y_sb
              + _sigmoid(ggdn_ref[...].astype(F32)) * y_gdn)
    x1 = x_ref[...] + _dot(merged.astype(BF16), wout_ref[...])
    x1_ref[...] = x1
    h2 = _rms(x1, fnw_ref[...]).astype(BF16)
    h2_ref[...] = h2
    q = _dot(h2, wq_ref[...]).astype(BF16)
    for hh in range(PEER_HEADS):
        base = hh * 2 * PEER_HALF
        s1_ref[hh] = _dot_nt(k1_ref[hh], q[:, base:base + PEER_HALF])
        s2_ref[hh] = _dot_nt(k2_ref[hh], q[:, base + PEER_HALF:base + 2 * PEER_HALF])


def _merge(o_sb, o_gdn, proj_main, x2d, w_sb, w_gdn, w_out, ffn_w, w_q, keys1, keys2, tm):
    n = x2d.shape[0]
    full = lambda shape: pl.BlockSpec(shape, lambda i: (0,) * len(shape))
    tok = lambda width, cb=0: pl.BlockSpec((tm, width), lambda i, cb=cb: (i, cb))
    st_spec = pl.BlockSpec((PEER_HEADS, PEER_N_KEYS, tm), lambda i: (0, 0, i))
    return pl.pallas_call(
        _merge_kernel,
        out_shape=(jax.ShapeDtypeStruct((n, D_MODEL), F32),
                   jax.ShapeDtypeStruct((n, D_MODEL), BF16),
                   jax.ShapeDtypeStruct((PEER_HEADS, PEER_N_KEYS, n), F32),
                   jax.ShapeDtypeStruct((PEER_HEADS, PEER_N_KEYS, n), F32)),
        grid=(n // tm,),
        in_specs=[
            tok(D_MODEL), tok(D_MODEL), tok(D_MODEL, 7), tok(D_MODEL, 8), tok(D_MODEL),
            full((D_MODEL, D_MODEL)), full((D_MODEL, D_MODEL)), full((D_MODEL, D_MODEL)),
            full((1, D_MODEL)), full((D_MODEL, 2 * PEER_HALF * PEER_HEADS)),
            full((PEER_HEADS, PEER_N_KEYS, PEER_HALF)), full((PEER_HEADS, PEER_N_KEYS, PEER_HALF)),
        ],
        out_specs=(tok(D_MODEL), tok(D_MODEL), st_spec, st_spec),
        compiler_params=pltpu.CompilerParams(
            dimension_semantics=("parallel",), vmem_limit_bytes=VMEM_LIMIT),
        name="merge",
    )(o_sb, o_gdn, proj_main, proj_main, x2d, w_sb, w_gdn, w_out, ffn_w, w_q, keys1, keys2)


def _top_ranked(s, k):
    assert k <= 16
    big = 2.0 ** 100
    s = jnp.maximum(s, -0.5 * big)
    vals = []
    for r in range(k):
        m = jnp.max(s, axis=0, keepdims=True)
        vals.append(m)
        s = jnp.where(s == m, -big * (1.0 + r / 16.0), s)
    rank = jnp.where(s <= -big, (s * (-1.0 / big) - 1.0) * 16.0, float(k))
    return jnp.concatenate(vals, axis=0), rank


def _topk_kernel(s1_ref, s2_ref, n1_ref, p1_ref, r2_ref, p2_ref, *, tl):
    kk = PEER_TOPK

    def sub(bi, _):
        l0 = pl.multiple_of(bi * LANES, LANES)
        s1 = s1_ref[0, :, pl.ds(l0, LANES)]
        s2 = s2_ref[0, :, pl.ds(l0, LANES)]
        v1, rank1 = _top_ranked(s1, kk)
        v2, rank2 = _top_ranked(s2, kk)
        jrow8 = lax.broadcasted_iota(jnp.int32, (8, LANES), 0)
        blocks = [v1[0:1] + v2]
        for i in range(1, 8):
            blocks.append(v1[i:i + 1] + jnp.where(jrow8 < kk // (i + 1), v2[0:8], NEG_INF))
        blocks.append(v1[8:16] + v2[0:1])
        cand = jnp.concatenate(blocks, axis=0)
        top = cand[0:1]
        work = cand
        tau = top
        for _ in range(kk):
            tau = jnp.max(work, axis=0, keepdims=True)
            work = jnp.where(work == tau, NEG_INF, work)
        sel = cand >= tau
        zsum = jnp.sum(jnp.where(sel, jnp.exp(jnp.where(sel, cand - top, 0.0)), 0.0),
                       axis=0, keepdims=True)
        counts = [jnp.sum(jnp.where(blocks[i] >= tau, 1.0, 0.0), axis=0, keepdims=True)
                  for i in range(8)]
        counts.append(jnp.where(blocks[8] >= tau, 1.0, 0.0))
        nrank = jnp.concatenate(counts, axis=0)
        n1 = jnp.zeros(s1.shape, F32)
        for i in range(kk):
            n1 = jnp.where(rank1 == float(i), nrank[i:i + 1], n1)
        n1_ref[0, :, pl.ds(l0, LANES)] = n1
        p1_ref[0, :, pl.ds(l0, LANES)] = jnp.exp(s1 - v1[0:1]) / zsum
        r2_ref[:, pl.ds(l0, LANES)] = rank2.astype(BF16)
        p2_ref[:, pl.ds(l0, LANES)] = jnp.exp(s2 - v2[0:1]).astype(BF16)
        return 0

    lax.fori_loop(0, tl // LANES, sub, 0)


def _peer_topk(s1t, s2t, tl):
    hh, kk, n = s1t.shape
    spec = pl.BlockSpec((1, kk, tl), lambda i, h: (h, 0, i))
    f32 = jax.ShapeDtypeStruct((hh, kk, n), F32)
    b16 = jax.ShapeDtypeStruct((hh * kk, n), BF16)
    spec2 = pl.BlockSpec((kk, tl), lambda i, h: (h, i))
    return pl.pallas_call(
        functools.partial(_topk_kernel, tl=tl),
        out_shape=(f32, f32, b16, b16),
        grid=(n // tl, hh),
        in_specs=[spec, spec],
        out_specs=(spec, spec, spec2, spec2),
        compiler_params=pltpu.CompilerParams(
            dimension_semantics=("parallel", "parallel"), vmem_limit_bytes=VMEM_LIMIT),
        name="peer_topk",
    )(s1t, s2t)


def _dense_kernel(*refs, ec):
    ns = DENSE_SPLIT
    u0_ref, h2n_ref = refs[:2]
    un_refs, vt_refs = refs[2:2 + ns], refs[2 + ns:2 + 2 * ns]
    (r2_ref, p2_ref, n1_ref, p1_ref, x1_ref, o_ref,
     acc_scr, w_scr, r2_scr, p2_scr, pre_a, pre_b) = refs[2 + 2 * ns:]
    i = pl.program_id(0)
    j = pl.program_id(1)
    nk = PEER_N_KEYS
    tl = h2n_ref.shape[1]
    pk = 16

    @pl.when(jnp.logical_and(i == 0, j == 0))
    def _():
        pre_a[...] = _dot(u0_ref[...], h2n_ref[...])

    @pl.when(j == 0)
    def _():
        acc_scr[...] = jnp.zeros_like(acc_scr)
        r2_scr[...] = r2_ref[...]
        p2_scr[...] = p2_ref[...]

    def step(pre_now, pre_next):
        u_next = jnp.concatenate([r[...] for r in un_refs], axis=0)
        pre_next[...] = _dot(u_next, h2n_ref[...])
        for e in range(ec // nk):
            e1 = j * (ec // nk) + e
            n1_rows = [jnp.broadcast_to(n1_ref[hh, pl.ds(e1, 1), :], (pk, tl)).astype(BF16)
                       for hh in range(PEER_HEADS)]
            p1_rows = [jnp.broadcast_to(p1_ref[hh, pl.ds(e1, 1), :], (pk, tl)).astype(BF16)
                       for hh in range(PEER_HEADS)]
            for tv in range(tl // LANES):
                lanes = slice(tv * LANES, (tv + 1) * LANES)
                g = None
                for hh in range(PEER_HEADS):
                    keys = slice(hh * nk, (hh + 1) * nk)
                    n1 = jnp.concatenate([n1_rows[hh][:, lanes]] * (nk // pk), axis=0)
                    p1 = jnp.concatenate([p1_rows[hh][:, lanes]] * (nk // pk), axis=0)
                    p2 = p2_scr[keys, lanes]
                    term = jnp.where(r2_scr[keys, lanes] < n1, p2, jnp.zeros_like(p2)) * p1
                    g = term if g is None else g + term
                blk = pre_now[e * nk:(e + 1) * nk, lanes].astype(BF16)
                act = (0.5 * blk) * (1.0 + lax.erf(blk * (2.0 ** -0.5)))
                w_scr[pl.ds(e * nk, nk), lanes] = act * g
        vt = jnp.concatenate([r[...] for r in vt_refs], axis=1)
        acc_scr[...] += _dot(vt, w_scr[...])

    @pl.when(j % 2 == 0)
    def _():
        step(pre_a, pre_b)

    @pl.when(j % 2 == 1)
    def _():
        step(pre_b, pre_a)

    @pl.when(j == pl.num_programs(1) - 1)
    def _():
        o_ref[...] = x1_ref[...] + acc_scr[...].T


def _peer_dense(h2t, u_tab, vt_tab, r2, p2, n1, p1, x1, tl, ec):
    d, n = h2t.shape
    ne = u_tab.shape[0]
    ni, nj = n // tl, ne // ec
    ns = DENSE_SPLIT
    assert nj % 2 == 0
    hspec = pl.BlockSpec((PEER_HEADS, PEER_N_KEYS, tl), lambda i, j: (0, 0, i))
    bspec = pl.BlockSpec((PEER_HEADS * PEER_N_KEYS, tl), lambda i, j: (0, i))
    return pl.pallas_call(
        functools.partial(_dense_kernel, ec=ec),
        out_shape=jax.ShapeDtypeStruct((n, d), F32),
        grid=(ni, nj),
        in_specs=[
            pl.BlockSpec((ec, d), lambda i, j: (0, 0)),
            pl.BlockSpec((d, tl), lambda i, j: (0, jnp.minimum(i + (j + 1) // nj, ni - 1))),
        ] + [
            pl.BlockSpec((ec // ns, d), lambda i, j, q=q: (((j + 1) % nj) * ns + q, 0)) for q in range(ns)
        ] + [
            pl.BlockSpec((d, ec // ns), lambda i, j, q=q: (0, j * ns + q)) for q in range(ns)
        ] + [
            bspec, bspec, hspec, hspec,
            pl.BlockSpec((tl, d), lambda i, j: (i, 0)),
        ],
        out_specs=pl.BlockSpec((tl, d), lambda i, j: (i, 0)),
        scratch_shapes=[pltpu.VMEM((d, tl), F32), pltpu.VMEM((ec, tl), BF16),
                        pltpu.VMEM((PEER_HEADS * PEER_N_KEYS, tl), BF16),
                        pltpu.VMEM((PEER_HEADS * PEER_N_KEYS, tl), BF16),
                        pltpu.VMEM((ec, tl), F32), pltpu.VMEM((ec, tl), F32)],
        compiler_params=pltpu.CompilerParams(
            dimension_semantics=("arbitrary", "arbitrary"), vmem_limit_bytes=VMEM_LIMIT),
        name="peer_dense",
    )(u_tab, h2t, *([u_tab] * ns), *([vt_tab] * ns), r2, p2, n1, p1, x1)


def _tile(n, pref):
    t = min(n, pref)
    assert n % t == 0, (n, t)
    return t


def kernel(x, mix_norm_w, w_in, sb_q_norm_w, sb_k_norm_w, gdn_conv_w, gdn_a_log, gdn_dt_bias,
           gdn_out_norm_w, w_branch_sb, w_branch_gdn, w_out, ffn_norm_w, peer_w_q, peer_keys1,
           peer_keys2, peer_u, peer_v):
    b, s, d = x.shape
    n = b * s
    depth = w_in.shape[0]
    sb_w = HEADS * HEAD_DIM
    small0 = 7 * sb_w
    small1 = small0 + 2 * HEADS
    nchunk = s // GDN_CHUNK
    x2d = x.reshape(n, d)
    for l in range(depth):
        w_main = jnp.concatenate([w_in[l][:, :small0], w_in[l][:, small1:]], axis=1).astype(BF16)
        w_small = jnp.pad(w_in[l][:, small0:small1], ((0, 0), (0, LANES - 2 * HEADS))).astype(BF16)
        proj_main, proj_small = _in_proj(x2d, mix_norm_w[l][None], w_main, w_small,
                                         _tile(n, 1024), 1024)
        proj3 = proj_main.reshape(b, s, MAIN_WIDTH)

        o_sb = _sb_attention(proj3, sb_q_norm_w[l][None], sb_k_norm_w[l][None], _tile(s, 256))

        ba = proj_small[:, :2 * HEADS].reshape(b, nchunk, GDN_CHUNK, 2 * HEADS)
        ba = ba.transpose(0, 3, 1, 2)
        b_col = ba[:, :HEADS, :, :, None]
        a_col = ba[:, HEADS:, :, :, None]
        a_row = ba[:, HEADS:, :, None, :]
        o_gdn = _gdn(proj3, gdn_conv_w[l], gdn_a_log[l], gdn_dt_bias[l], a_col, b_col, a_row,
                     gdn_out_norm_w[l][None])

        x1, h2, s1t, s2t = _merge(
            o_sb.reshape(n, sb_w), o_gdn.reshape(n, sb_w), proj_main, x2d,
            w_branch_sb[l].astype(BF16), w_branch_gdn[l].astype(BF16), w_out[l].astype(BF16),
            ffn_norm_w[l][None], peer_w_q[l].astype(BF16),
            peer_keys1[l].astype(BF16), peer_keys2[l].astype(BF16), _tile(n, 256))

        n1, p1, r2, p2 = _peer_topk(s1t, s2t, _tile(n, 1024))
        x2d = _peer_dense(h2.T, peer_u[l].astype(BF16), peer_v[l].astype(BF16).T,
                          r2, p2, n1, p1, x1, _tile(n, 512), 1024)
    return x2d.reshape(b, s, d)
```

```python
import functools

import jax
import jax.numpy as jnp
import numpy as np
from jax import lax
from jax.experimental import pallas as pl
from jax.experimental.pallas import tpu as pltpu

F32 = jnp.float32
BF16 = jnp.bfloat16

D_MODEL = 1024
HEADS = 8
HEAD_DIM = 128
LANES = 128
GDN_CONV = 4
GDN_CHUNK = 64
PEER_HEADS = 8
PEER_N_KEYS = 128
PEER_HALF = 128
PEER_TOPK = 16
EPS = 1e-6
MAIN_WIDTH = 9 * D_MODEL
VMEM_LIMIT = 56 * 1024 * 1024
SB_LOG_UNDERFLOW = -104.0
SB_GROUP = 8
GDN_GROUP = 2
GDN_UNROLL = 8
NEG_INF = float("-inf")
POS_INF = float("inf")


def _sigmoid(x):
    return 1.0 / (1.0 + jnp.exp(-x))


def _softplus(x):
    return jnp.maximum(x, 0.0) + jnp.log1p(jnp.exp(-jnp.abs(x)))


def _rms(x, w):
    return x * lax.rsqrt(jnp.mean(x * x, axis=-1, keepdims=True) + EPS) * w


def _dot(a, b):
    return jnp.dot(a, b, preferred_element_type=F32)


def _dot_nt(a, b):
    return lax.dot_general(a, b, (((1,), (1,)), ((), ())), preferred_element_type=F32)


def _dot_tn(a, b):
    return lax.dot_general(a, b, (((0,), (0,)), ((), ())), preferred_element_type=F32)


def _inproj_kernel(x_ref, nw_ref, w_ref, ws_ref, o_ref, os_ref, h_scr):
    j = pl.program_id(1)

    @pl.when(j == 0)
    def _():
        h = _rms(x_ref[...], nw_ref[...]).astype(BF16)
        h_scr[...] = h
        os_ref[...] = _dot(h, ws_ref[...])

    o_ref[...] = _dot(h_scr[...], w_ref[...]).astype(o_ref.dtype)


def _in_proj(x2d, norm_w, w_main, w_small, tm, tn):
    n = x2d.shape[0]
    return pl.pallas_call(
        _inproj_kernel,
        out_shape=(jax.ShapeDtypeStruct((n, MAIN_WIDTH), BF16),
                   jax.ShapeDtypeStruct((n, LANES), F32)),
        grid=(n // tm, MAIN_WIDTH // tn),
        in_specs=[
            pl.BlockSpec((tm, D_MODEL), lambda i, j: (i, 0)),
            pl.BlockSpec((1, D_MODEL), lambda i, j: (0, 0)),
            pl.BlockSpec((D_MODEL, tn), lambda i, j: (0, j)),
            pl.BlockSpec((D_MODEL, LANES), lambda i, j: (0, 0)),
        ],
        out_specs=(pl.BlockSpec((tm, tn), lambda i, j: (i, j)),
                   pl.BlockSpec((tm, LANES), lambda i, j: (i, 0))),
        scratch_shapes=[pltpu.VMEM((tm, D_MODEL), BF16)],
        compiler_params=pltpu.CompilerParams(
            dimension_semantics=("parallel", "arbitrary"), vmem_limit_bytes=VMEM_LIMIT),
        name="in_proj",
    )(x2d, norm_w, w_main, w_small)


def _sb_kernel(q_ref, k_ref, v_ref, qw_ref, kw_ref, tri_ref, o_ref, kn_scr, *, tq):
    i = pl.program_id(2)
    kb = LANES
    dh = HEAD_DIM
    heads = range(SB_GROUP)

    @pl.when(i == 0)
    def _():
        for g in heads:
            kn_scr[:, g * dh:(g + 1) * dh] = _rms(
                k_ref[0, :, g * dh:(g + 1) * dh].astype(F32), kw_ref[...]).astype(BF16)

    scale = dh ** -0.5
    qn = [(_rms(q_ref[0, :, g * dh:(g + 1) * dh].astype(F32), qw_ref[...]) * scale).astype(BF16)
          for g in heads]
    row = i * tq + lax.broadcasted_iota(jnp.int32, (tq, kb), 0)
    col0 = lax.broadcasted_iota(jnp.int32, (tq, kb), 1)
    npair = (i + 1) * (tq // (2 * kb))

    def cond(carry):
        p, cs, _ = carry
        cmax = cs[0]
        for g in heads[1:]:
            cmax = jnp.maximum(cmax, cs[g])
        return jnp.logical_and(p < npair, jnp.max(cmax) > SB_LOG_UNDERFLOW)

    def body(carry):
        p, cs, accs = carry
        k0 = pl.multiple_of((npair - 1 - p) * (2 * kb), 2 * kb)
        masks = [(col0 + (k0 + half * kb)) < row for half in range(2)]
        z2 = [_dot_nt(qn[g], kn_scr[pl.ds(k0, 2 * kb), g * dh:(g + 1) * dh]) for g in heads]
        cats, logits = [], []
        for g in heads:
            cat_g, logit_g = [], []
            for half in range(2):
                z = z2[g][:, half * kb:(half + 1) * kb]
                sp = jnp.maximum(z, 0.0) + jnp.log(1.0 + jnp.exp(-jnp.abs(z)))
                lneg = jnp.where(masks[half], -sp, 0.0)
                hi = lneg.astype(BF16)
                lo = (lneg - hi.astype(F32)).astype(BF16)
                cat_g.append(jnp.concatenate([hi, lo], axis=1))
                logit_g.append(z - sp)
            cats.append(jnp.concatenate(cat_g, axis=0))
            logits.append(logit_g)
        res = [_dot(cats[g], tri_ref[...]) for g in heads]
        a2, c_new = [], []
        for g in heads:
            c_early = cs[g] + res[g][tq:, kb:]
            a_late = jnp.where(masks[1], jnp.exp(logits[g][1] + res[g][tq:, :kb] + cs[g]), 0.0)
            a_early = jnp.where(masks[0], jnp.exp(logits[g][0] + res[g][:tq, :kb] + c_early), 0.0)
            a2.append(jnp.concatenate([a_early, a_late], axis=1).astype(BF16))
            c_new.append(c_early + res[g][:tq, kb:])
        accs = tuple(accs[g] + _dot(a2[g], v_ref[0, pl.ds(k0, 2 * kb), g * dh:(g + 1) * dh])
                     for g in heads)
        return p + 1, tuple(c_new), accs

    init = (jnp.int32(0), tuple(jnp.zeros((tq, kb), F32) for _ in heads),
            tuple(jnp.zeros((tq, dh), F32) for _ in heads))
    _, _, accs = lax.while_loop(cond, body, init)
    for g in heads:
        o_ref[0, :, g * dh:(g + 1) * dh] = accs[g].astype(o_ref.dtype)


def _sb_tri():
    r = np.arange(2 * LANES)[:, None] % LANES
    c = np.arange(2 * LANES)[None, :]
    m = np.where(c < LANES, r > c, True)
    return jnp.asarray(m, dtype=BF16)


def _sb_attention(proj3, qw, kw, tq):
    b, s, _ = proj3.shape
    w = SB_GROUP * HEAD_DIM
    ng = HEADS // SB_GROUP
    return pl.pallas_call(
        functools.partial(_sb_kernel, tq=tq),
        out_shape=jax.ShapeDtypeStruct((b, s, HEADS * HEAD_DIM), BF16),
        grid=(b, ng, s // tq),
        in_specs=[
            pl.BlockSpec((1, tq, w), lambda bi, h, i: (bi, i, h)),
            pl.BlockSpec((1, s, w), lambda bi, h, i: (bi, 0, ng + h)),
            pl.BlockSpec((1, s, w), lambda bi, h, i: (bi, 0, 2 * ng + h)),
            pl.BlockSpec((1, HEAD_DIM), lambda bi, h, i: (0, 0)),
            pl.BlockSpec((1, HEAD_DIM), lambda bi, h, i: (0, 0)),
            pl.BlockSpec((2 * LANES, 2 * LANES), lambda bi, h, i: (0, 0)),
        ],
        out_specs=pl.BlockSpec((1, tq, w), lambda bi, h, i: (bi, i, h)),
        scratch_shapes=[pltpu.VMEM((s, w), BF16)],
        compiler_params=pltpu.CompilerParams(
            dimension_semantics=("parallel", "parallel", "arbitrary"),
            vmem_limit_bytes=VMEM_LIMIT),
        name="sb_attn",
    )(proj3, proj3, proj3, qw, kw, _sb_tri())


def _gdn_kernel(alog_ref, dtb_ref, gq_ref, gk_ref, gv_ref, z_ref, cwq_ref, cwk_ref, cwv_ref,
                acol_ref, bcol_ref, arow_ref, onw_ref, o_ref,
                xpad, qs, ks, vs, m_scr, b_scr, qp_scr, op_scr, eg_scr):
    hg = pl.program_id(1)
    s = qs.shape[0]
    c = GDN_CHUNK
    dk = HEAD_DIM
    w = GDN_GROUP * dk
    pad = 8

    def conv_silu(src_ref, cw_ref):
        xpad[pl.ds(0, pad), :] = jnp.zeros((pad, w), F32)
        xpad[pl.ds(pad, s), :] = src_ref[0].astype(F32)
        y = cw_ref[0:1, :] * xpad[pl.ds(pad - 3, s), :]
        for t in range(1, GDN_CONV):
            y = y + cw_ref[t:t + 1, :] * xpad[pl.ds(pad - 3 + t, s), :]
        return y * _sigmoid(y)

    def l2n(x):
        return x * lax.rsqrt(jnp.sum(x * x, axis=-1, keepdims=True) + EPS)

    yq = conv_silu(gq_ref, cwq_ref)
    for g in range(GDN_GROUP):
        qs[:, g * dk:(g + 1) * dk] = l2n(yq[:, g * dk:(g + 1) * dk]) * (dk ** -0.5)
    yk = conv_silu(gk_ref, cwk_ref)
    for g in range(GDN_GROUP):
        ks[:, g * dk:(g + 1) * dk] = l2n(yk[:, g * dk:(g + 1) * dk])
    vs[...] = conv_silu(gv_ref, cwv_ref)

    a_gain = [jnp.exp(jnp.full((1, 1), alog_ref[hg * GDN_GROUP + g], F32)) for g in range(GDN_GROUP)]
    dtb = [jnp.full((1, 1), dtb_ref[hg * GDN_GROUP + g], F32) for g in range(GDN_GROUP)]

    ri = lax.broadcasted_iota(jnp.int32, (c, c), 0)
    ci = lax.broadcasted_iota(jnp.int32, (c, c), 1)
    tril = ri >= ci
    strict = ri > ci
    tril_b = jnp.where(tril, 1.0, 0.0).astype(BF16)
    triu_b = jnp.where(ri <= ci, 1.0, 0.0).astype(BF16)
    eye = jnp.where(ri == ci, 1.0, 0.0)

    def split(x):
        hi = x.astype(BF16)
        return hi, (x - hi.astype(F32)).astype(BF16)

    def prepare(nn):
        pairs = [(g, nn * GDN_UNROLL + j) for j in range(GDN_UNROLL) for g in range(GDN_GROUP)]
        ps = range(len(pairs))
        rows = [pl.ds(pl.multiple_of(n * c, c), c) for _, n in pairs]
        q = [qs[rows[i], g * dk:(g + 1) * dk] for i, (g, _) in enumerate(pairs)]
        k = [ks[rows[i], g * dk:(g + 1) * dk] for i, (g, _) in enumerate(pairs)]
        v = [vs[rows[i], g * dk:(g + 1) * dk] for i, (g, _) in enumerate(pairs)]
        g_col = [-a_gain[g] * _softplus(acol_ref[0, g, n] + dtb[g]) for g, n in pairs]
        g_row = [-a_gain[g] * _softplus(arow_ref[0, g, n] + dtb[g]) for g, n in pairs]
        beta = [jnp.broadcast_to(_sigmoid(bcol_ref[0, g, n]), (c, dk)) for g, n in pairs]
        gcs = [split(jnp.broadcast_to(g_col[i], (c, dk))) for i in ps]
        gc = [_dot(tril_b, gcs[i][0]) + _dot(tril_b, gcs[i][1]) for i in ps]
        grs = [split(jnp.broadcast_to(g_row[i], (8, c))) for i in ps]
        gc_row = [(_dot(grs[i][0], triu_b) + _dot(grs[i][1], triu_b))[0:1] for i in ps]
        yield
        decay = [jnp.where(tril, jnp.exp(jnp.where(tril, gc[i][:, :c] - gc_row[i], 0.0)), 0.0)
                 for i in ps]
        kbf = [k[i].astype(BF16) for i in ps]
        k_beta = [k[i] * beta[i] for i in ps]
        u = [-jnp.where(strict, _dot_nt(k_beta[i].astype(BF16), kbf[i]) * decay[i], 0.0) for i in ps]
        mb = [u[i].astype(BF16) for i in ps]
        yield
        for _ in range(5):
            m2 = [_dot(mb[i], mb[i]) for i in ps]
            mb = [m2[i].astype(BF16) for i in ps]
            u = [u[i] + m2[i] + _dot(u[i].astype(BF16), mb[i]) for i in ps]
            yield
        tb = [(eye + u[i]).astype(BF16) for i in ps]
        egc = [jnp.exp(gc[i]) for i in ps]
        vc_b = [_dot(tb[i], (v[i] * beta[i]).astype(BF16)).astype(BF16) for i in ps]
        kcd_b = [_dot(tb[i], (k_beta[i] * egc[i]).astype(BF16)).astype(BF16) for i in ps]
        attn_b = [jnp.where(tril, _dot_nt(q[i].astype(BF16), kbf[i]) * decay[i], 0.0).astype(BF16)
                  for i in ps]
        yield
        g_last = [gc[i][c - 1:c, :] for i in ps]
        kw_b = [(k[i] * jnp.exp(g_last[i] - gc[i])).astype(BF16) for i in ps]
        m_out = [_dot_tn(kw_b[i], kcd_b[i]).astype(BF16) for i in ps]
        b_out = [_dot_tn(kw_b[i], vc_b[i]) for i in ps]
        qp_out = [(q[i] * egc[i] - _dot(attn_b[i], kcd_b[i])).astype(BF16) for i in ps]
        op_out = [_dot(attn_b[i], vc_b[i]) for i in ps]
        for i, (g, n) in enumerate(pairs):
            m_scr[g, n] = m_out[i]
            b_scr[g, n] = b_out[i]
            qp_scr[g, n] = qp_out[i]
            op_scr[g, n] = op_out[i]
            eg_scr[g, n] = jnp.broadcast_to(jnp.exp(g_last[i]), (8, dk))

    def scan(n, states):
        r0 = pl.multiple_of(n * c, c)
        gs = range(GDN_GROUP)
        sb = [states[g].astype(BF16) for g in gs]
        zc = [z_ref[0, pl.ds(r0, c), g * dk:(g + 1) * dk].astype(F32) for g in gs]
        o = [_dot(qp_scr[g, n], sb[g]) + op_scr[g, n] for g in gs]
        new = tuple(states[g] * eg_scr[g, n][0:1] - _dot(m_scr[g, n], sb[g]) + b_scr[g, n]
                    for g in gs)
        for g in gs:
            o_ref[0, pl.ds(r0, c), g * dk:(g + 1) * dk] = (
                _rms(o[g], onw_ref[...]) * (zc[g] * _sigmoid(zc[g]))).astype(o_ref.dtype)
        return new

    ngroup = s // (c * GDN_UNROLL)
    for _ in prepare(0):
        pass

    def body(nn, states):
        stages = prepare(nn)
        for step in range(GDN_UNROLL):
            next(stages, None)
            states = scan((nn - 1) * GDN_UNROLL + step, states)
        for _ in stages:
            pass
        return states

    states = lax.fori_loop(1, ngroup, body,
                           tuple(jnp.zeros((dk, dk), F32) for _ in range(GDN_GROUP)))
    lax.fori_loop((ngroup - 1) * GDN_UNROLL, ngroup * GDN_UNROLL, scan, states)


def _gdn(proj3, conv_w, a_log, dt_bias, a_col, b_col, a_row, out_norm_w):
    b, s, _ = proj3.shape
    n = s // GDN_CHUNK
    c = GDN_CHUNK
    gg = GDN_GROUP
    w = gg * HEAD_DIM
    assert s % (c * GDN_UNROLL) == 0, s
    col = lambda base: pl.BlockSpec((1, s, w), lambda bi, h: (bi, 0, base // gg + h))
    cw = lambda base: pl.BlockSpec((GDN_CONV, w), lambda bi, h: (0, base // gg + h))
    smem = pl.BlockSpec(memory_space=pltpu.SMEM)
    return pl.pallas_call(
        _gdn_kernel,
        out_shape=jax.ShapeDtypeStruct((b, s, HEADS * HEAD_DIM), BF16),
        grid=(b, HEADS // gg),
        in_specs=[
            smem, smem,
            col(3 * HEADS), col(4 * HEADS), col(5 * HEADS), col(6 * HEADS),
            cw(0), cw(HEADS), cw(2 * HEADS),
            pl.BlockSpec((1, gg, n, c, 1), lambda bi, h: (bi, h, 0, 0, 0)),
            pl.BlockSpec((1, gg, n, c, 1), lambda bi, h: (bi, h, 0, 0, 0)),
            pl.BlockSpec((1, gg, n, 1, c), lambda bi, h: (bi, h, 0, 0, 0)),
            pl.BlockSpec((1, HEAD_DIM), lambda bi, h: (0, 0)),
        ],
        out_specs=pl.BlockSpec((1, s, w), lambda bi, h: (bi, 0, h)),
        scratch_shapes=[
            pltpu.VMEM((s + 8, w), F32), pltpu.VMEM((s, w), F32), pltpu.VMEM((s, w), F32),
            pltpu.VMEM((s, w), F32),
            pltpu.VMEM((gg, n, HEAD_DIM, HEAD_DIM), BF16), pltpu.VMEM((gg, n, HEAD_DIM, HEAD_DIM), F32),
            pltpu.VMEM((gg, n, c, HEAD_DIM), BF16), pltpu.VMEM((gg, n, c, HEAD_DIM), F32),
            pltpu.VMEM((gg, n, 8, HEAD_DIM), F32),
        ],
        compiler_params=pltpu.CompilerParams(
            dimension_semantics=("parallel", "parallel"), vmem_limit_bytes=VMEM_LIMIT),
        name="gdn",
    )(a_log, dt_bias, proj3, proj3, proj3, proj3, conv_w, conv_w, conv_w,
      a_col, b_col, a_row, out_norm_w)


def _merge_kernel(osb_ref, ogdn_ref, gsb_ref, ggdn_ref, x_ref, wsb_ref, wgdn_ref, wout_ref,
                  fnw_ref, wq_ref, k1_ref, k2_ref, x1_ref, h2_ref, s1_ref, s2_ref):
    y_sb = _dot(osb_ref[...], wsb_ref[...])
    y_gdn = _dot(ogdn_ref[...], wgdn_ref[...])
    merged = (_sigmoid(gsb_ref[...].astype(F32)) * y_sb
              + _sigmoid(ggdn_ref[...].astype(F32)) * y_gdn)
    x1 = x_ref[...] + _dot(merged.astype(BF16), wout_ref[...])
    x1_ref[...] = x1
    h2 = _rms(x1, fnw_ref[...]).astype(BF16)
    h2_ref[...] = h2
    q = _dot(h2, wq_ref[...]).astype(BF16)
    for hh in range(PEER_HEADS):
        base = hh * 2 * PEER_HALF
        s1_ref[hh] = _dot_nt(k1_ref[hh], q[:, base:base + PEER_HALF])
        s2_ref[hh] = _dot_nt(k2_ref[hh], q[:, base + PEER_HALF:base + 2 * PEER_HALF])


def _merge(o_sb, o_gdn, proj_main, x2d, w_sb, w_gdn, w_out, ffn_w, w_q, keys1, keys2, tm):
    n = x2d.shape[0]
    full = lambda shape: pl.BlockSpec(shape, lambda i: (0,) * len(shape))
    tok = lambda width, cb=0: pl.BlockSpec((tm, width), lambda i, cb=cb: (i, cb))
    st_spec = pl.BlockSpec((PEER_HEADS, PEER_N_KEYS, tm), lambda i: (0, 0, i))
    return pl.pallas_call(
        _merge_kernel,
        out_shape=(jax.ShapeDtypeStruct((n, D_MODEL), F32),
                   jax.ShapeDtypeStruct((n, D_MODEL), BF16),
                   jax.ShapeDtypeStruct((PEER_HEADS, PEER_N_KEYS, n), F32),
                   jax.ShapeDtypeStruct((PEER_HEADS, PEER_N_KEYS, n), F32)),
        grid=(n // tm,),
        in_specs=[
            tok(D_MODEL), tok(D_MODEL), tok(D_MODEL, 7), tok(D_MODEL, 8), tok(D_MODEL),
            full((D_MODEL, D_MODEL)), full((D_MODEL, D_MODEL)), full((D_MODEL, D_MODEL)),
            full((1, D_MODEL)), full((D_MODEL, 2 * PEER_HALF * PEER_HEADS)),
            full((PEER_HEADS, PEER_N_KEYS, PEER_HALF)), full((PEER_HEADS, PEER_N_KEYS, PEER_HALF)),
        ],
        out_specs=(tok(D_MODEL), tok(D_MODEL), st_spec, st_spec),
        compiler_params=pltpu.CompilerParams(
            dimension_semantics=("parallel",), vmem_limit_bytes=VMEM_LIMIT),
        name="merge",
    )(o_sb, o_gdn, proj_main, proj_main, x2d, w_sb, w_gdn, w_out, ffn_w, w_q, keys1, keys2)


def _top_ranked(s, k):
    assert k <= 16
    big = 2.0 ** 100
    s = jnp.maximum(s, -0.5 * big)
    vals = []
    for r in range(k):
        m = jnp.max(s, axis=0, keepdims=True)
        vals.append(m)
        s = jnp.where(s == m, -big * (1.0 + r / 16.0), s)
    rank = jnp.where(s <= -big, (s * (-1.0 / big) - 1.0) * 16.0, float(k))
    return jnp.concatenate(vals, axis=0), rank


def _topk_kernel(s1_ref, s2_ref, n1_ref, p1_ref, r2_ref, p2_ref, *, tl):
    kk = PEER_TOPK

    def sub(bi, _):
        l0 = pl.multiple_of(bi * LANES, LANES)
        s1 = s1_ref[0, :, pl.ds(l0, LANES)]
        s2 = s2_ref[0, :, pl.ds(l0, LANES)]
        v1, rank1 = _top_ranked(s1, kk)
        v2, rank2 = _top_ranked(s2, kk)
        jrow8 = lax.broadcasted_iota(jnp.int32, (8, LANES), 0)
        blocks = [v1[0:1] + v2]
        for i in range(1, 8):
            blocks.append(v1[i:i + 1] + jnp.where(jrow8 < kk // (i + 1), v2[0:8], NEG_INF))
        blocks.append(v1[8:16] + v2[0:1])
        cand = jnp.concatenate(blocks, axis=0)
        top = cand[0:1]
        work = cand
        tau = top
        for _ in range(kk):
            tau = jnp.max(work, axis=0, keepdims=True)
            work = jnp.where(work == tau, NEG_INF, work)
        sel = cand >= tau
        zsum = jnp.sum(jnp.where(sel, jnp.exp(jnp.where(sel, cand - top, 0.0)), 0.0),
                       axis=0, keepdims=True)
        counts = [jnp.sum(jnp.where(blocks[i] >= tau, 1.0, 0.0), axis=0, keepdims=True)
                  for i in range(8)]
        counts.append(jnp.where(blocks[8] >= tau, 1.0, 0.0))
        nrank = jnp.concatenate(counts, axis=0)
        n1 = jnp.zeros(s1.shape, F32)
        for i in range(kk):
            n1 = jnp.where(rank1 == float(i), nrank[i:i + 1], n1)
        n1_ref[0, :, pl.ds(l0, LANES)] = n1
        p1_ref[0, :, pl.ds(l0, LANES)] = jnp.exp(s1 - v1[0:1]) / zsum
        r2_ref[:, pl.ds(l0, LANES)] = rank2.astype(BF16)
        p2_ref[:, pl.ds(l0, LANES)] = jnp.exp(s2 - v2[0:1]).astype(BF16)
        return 0

    lax.fori_loop(0, tl // LANES, sub, 0)


def _peer_topk(s1t, s2t, tl):
    hh, kk, n = s1t.shape
    spec = pl.BlockSpec((1, kk, tl), lambda i, h: (h, 0, i))
    f32 = jax.ShapeDtypeStruct((hh, kk, n), F32)
    b16 = jax.ShapeDtypeStruct((hh * kk, n), BF16)
    spec2 = pl.BlockSpec((kk, tl), lambda i, h: (h, i))
    return pl.pallas_call(
        functools.partial(_topk_kernel, tl=tl),
        out_shape=(f32, f32, b16, b16),
        grid=(n // tl, hh),
        in_specs=[spec, spec],
        out_specs=(spec, spec, spec2, spec2),
        compiler_params=pltpu.CompilerParams(
            dimension_semantics=("parallel", "parallel"), vmem_limit_bytes=VMEM_LIMIT),
        name="peer_topk",
    )(s1t, s2t)


def _dense_kernel(h2t_ref, u_ref, vt_ref, r2_ref, p2_ref, n1_ref, p1_ref, x1_ref, o_ref,
                  acc_scr, w_scr, r2_scr, p2_scr, *, ec):
    j = pl.program_id(1)
    nk = PEER_N_KEYS
    tl = h2t_ref.shape[1]
    pk = 16

    @pl.when(j == 0)
    def _():
        acc_scr[...] = jnp.zeros_like(acc_scr)
        r2_scr[...] = r2_ref[...]
        p2_scr[...] = p2_ref[...]

    pre = _dot(u_ref[...], h2t_ref[...])
    for e in range(ec // nk):
        e1 = j * (ec // nk) + e
        n1_rows = [jnp.broadcast_to(n1_ref[hh, pl.ds(e1, 1), :], (pk, tl)).astype(BF16)
                   for hh in range(PEER_HEADS)]
        p1_rows = [jnp.broadcast_to(p1_ref[hh, pl.ds(e1, 1), :], (pk, tl)).astype(BF16)
                   for hh in range(PEER_HEADS)]
        for tv in range(tl // LANES):
            lanes = slice(tv * LANES, (tv + 1) * LANES)
            g = None
            for hh in range(PEER_HEADS):
                keys = slice(hh * nk, (hh + 1) * nk)
                n1 = jnp.concatenate([n1_rows[hh][:, lanes]] * (nk // pk), axis=0)
                p1 = jnp.concatenate([p1_rows[hh][:, lanes]] * (nk // pk), axis=0)
                p2 = p2_scr[keys, lanes]
                term = jnp.where(r2_scr[keys, lanes] < n1, p2, jnp.zeros_like(p2)) * p1
                g = term if g is None else g + term
            blk = pre[e * nk:(e + 1) * nk, lanes].astype(BF16)
            act = (0.5 * blk) * (1.0 + lax.erf(blk * (2.0 ** -0.5)))
            w_scr[pl.ds(e * nk, nk), lanes] = act * g
    acc_scr[...] += _dot(vt_ref[...], w_scr[...])

    @pl.when(j == pl.num_programs(1) - 1)
    def _():
        o_ref[...] = x1_ref[...] + acc_scr[...].T


def _peer_dense(h2t, u_tab, vt_tab, r2, p2, n1, p1, x1, tl, ec):
    d, n = h2t.shape
    ne = u_tab.shape[0]
    hspec = pl.BlockSpec((PEER_HEADS, PEER_N_KEYS, tl), lambda i, j: (0, 0, i))
    bspec = pl.BlockSpec((PEER_HEADS * PEER_N_KEYS, tl), lambda i, j: (0, i))
    return pl.pallas_call(
        functools.partial(_dense_kernel, ec=ec),
        out_shape=jax.ShapeDtypeStruct((n, d), F32),
        grid=(n // tl, ne // ec),
        in_specs=[
            pl.BlockSpec((d, tl), lambda i, j: (0, i)),
            pl.BlockSpec((ec, d), lambda i, j: (j, 0)),
            pl.BlockSpec((d, ec), lambda i, j: (0, j)),
            bspec, bspec, hspec, hspec,
            pl.BlockSpec((tl, d), lambda i, j: (i, 0)),
        ],
        out_specs=pl.BlockSpec((tl, d), lambda i, j: (i, 0)),
        scratch_shapes=[pltpu.VMEM((d, tl), F32), pltpu.VMEM((ec, tl), BF16),
                        pltpu.VMEM((PEER_HEADS * PEER_N_KEYS, tl), BF16),
                        pltpu.VMEM((PEER_HEADS * PEER_N_KEYS, tl), BF16)],
        compiler_params=pltpu.CompilerParams(
            dimension_semantics=("parallel", "arbitrary"), vmem_limit_bytes=VMEM_LIMIT),
        name="peer_dense",
    )(h2t, u_tab, vt_tab, r2, p2, n1, p1, x1)


def _tile(n, pref):
    t = min(n, pref)
    assert n % t == 0, (n, t)
    return t


def kernel(x, mix_norm_w, w_in, sb_q_norm_w, sb_k_norm_w, gdn_conv_w, gdn_a_log, gdn_dt_bias,
           gdn_out_norm_w, w_branch_sb, w_branch_gdn, w_out, ffn_norm_w, peer_w_q, peer_keys1,
           peer_keys2, peer_u, peer_v):
    b, s, d = x.shape
    n = b * s
    depth = w_in.shape[0]
    sb_w = HEADS * HEAD_DIM
    small0 = 7 * sb_w
    small1 = small0 + 2 * HEADS
    nchunk = s // GDN_CHUNK
    x2d = x.reshape(n, d)
    for l in range(depth):
        w_main = jnp.concatenate([w_in[l][:, :small0], w_in[l][:, small1:]], axis=1).astype(BF16)
        w_small = jnp.pad(w_in[l][:, small0:small1], ((0, 0), (0, LANES - 2 * HEADS))).astype(BF16)
        proj_main, proj_small = _in_proj(x2d, mix_norm_w[l][None], w_main, w_small,
                                         _tile(n, 1024), 1024)
        proj3 = proj_main.reshape(b, s, MAIN_WIDTH)

        o_sb = _sb_attention(proj3, sb_q_norm_w[l][None], sb_k_norm_w[l][None], _tile(s, 256))

        ba = proj_small[:, :2 * HEADS].reshape(b, nchunk, GDN_CHUNK, 2 * HEADS)
        ba = ba.transpose(0, 3, 1, 2)
        b_col = ba[:, :HEADS, :, :, None]
        a_col = ba[:, HEADS:, :, :, None]
        a_row = ba[:, HEADS:, :, None, :]
        o_gdn = _gdn(proj3, gdn_conv_w[l], gdn_a_log[l], gdn_dt_bias[l], a_col, b_col, a_row,
                     gdn_out_norm_w[l][None])

        x1, h2, s1t, s2t = _merge(
            o_sb.reshape(n, sb_w), o_gdn.reshape(n, sb_w), proj_main, x2d,
            w_branch_sb[l].astype(BF16), w_branch_gdn[l].astype(BF16), w_out[l].astype(BF16),
            ffn_norm_w[l][None], peer_w_q[l].astype(BF16),
            peer_keys1[l].astype(BF16), peer_keys2[l].astype(BF16), _tile(n, 256))

        n1, p1, r2, p2 = _peer_topk(s1t, s2t, _tile(n, 1024))
        x2d = _peer_dense(h2.T, peer_u[l].astype(BF16), peer_v[l].astype(BF16).T,
                          r2, p2, n1, p1, x1, _tile(n, 512), 2048)
    return x2d.reshape(b, s, d)
```

```python
import functools

import jax
import jax.numpy as jnp
import numpy as np
from jax import lax
from jax.experimental import pallas as pl
from jax.experimental.pallas import tpu as pltpu

F32 = jnp.float32
BF16 = jnp.bfloat16

D_MODEL = 1024
HEADS = 8
HEAD_DIM = 128
LANES = 128
GDN_CONV = 4
GDN_CHUNK = 64
PEER_HEADS = 8
PEER_N_KEYS = 128
PEER_HALF = 128
PEER_TOPK = 16
EPS = 1e-6
MAIN_WIDTH = 9 * D_MODEL
VMEM_LIMIT = 56 * 1024 * 1024
SB_LOG_UNDERFLOW = -104.0
SB_GROUP = 8
GDN_GROUP = 2
GDN_UNROLL = 8
IN_PROJ_ROWS, IN_PROJ_COLS = 1024, 2304
SB_QUERY_ROWS = 256
MERGE_ROWS = 256
TOPK_TOKENS = 1024
DENSE_TOKENS, DENSE_EXPERTS = 512, 2048
NEG_INF = float("-inf")
POS_INF = float("inf")


def _sigmoid(x):
    return 1.0 / (1.0 + jnp.exp(-x))


def _softplus(x):
    return jnp.maximum(x, 0.0) + jnp.log1p(jnp.exp(-jnp.abs(x)))


def _rms(x, w):
    return x * lax.rsqrt(jnp.mean(x * x, axis=-1, keepdims=True) + EPS) * w


def _dot(a, b):
    return jnp.dot(a, b, preferred_element_type=F32)


def _dot_nt(a, b):
    return lax.dot_general(a, b, (((1,), (1,)), ((), ())), preferred_element_type=F32)


def _dot_tn(a, b):
    return lax.dot_general(a, b, (((0,), (0,)), ((), ())), preferred_element_type=F32)


def _inproj_kernel(x_ref, nw_ref, w_ref, ws_ref, o_ref, os_ref, h_scr):
    j = pl.program_id(1)

    @pl.when(j == 0)
    def _():
        h = _rms(x_ref[...], nw_ref[...]).astype(BF16)
        h_scr[...] = h
        os_ref[...] = _dot(h, ws_ref[...])

    o_ref[...] = _dot(h_scr[...], w_ref[...]).astype(o_ref.dtype)


def _in_proj(x2d, norm_w, w_main, w_small, tm, tn):
    n = x2d.shape[0]
    return pl.pallas_call(
        _inproj_kernel,
        out_shape=(jax.ShapeDtypeStruct((n, MAIN_WIDTH), BF16),
                   jax.ShapeDtypeStruct((n, LANES), F32)),
        grid=(n // tm, MAIN_WIDTH // tn),
        in_specs=[
            pl.BlockSpec((tm, D_MODEL), lambda i, j: (i, 0)),
            pl.BlockSpec((1, D_MODEL), lambda i, j: (0, 0)),
            pl.BlockSpec((D_MODEL, tn), lambda i, j: (0, j)),
            pl.BlockSpec((D_MODEL, LANES), lambda i, j: (0, 0)),
        ],
        out_specs=(pl.BlockSpec((tm, tn), lambda i, j: (i, j)),
                   pl.BlockSpec((tm, LANES), lambda i, j: (i, 0))),
        scratch_shapes=[pltpu.VMEM((tm, D_MODEL), BF16)],
        compiler_params=pltpu.CompilerParams(
            dimension_semantics=("parallel", "arbitrary"), vmem_limit_bytes=VMEM_LIMIT),
        name="in_proj",
    )(x2d, norm_w, w_main, w_small)


def _sb_kernel(q_ref, k_ref, v_ref, qw_ref, kw_ref, tri_ref, o_ref, kn_scr, *, tq):
    i = pl.program_id(2)
    kb = LANES
    dh = HEAD_DIM
    heads = range(SB_GROUP)

    @pl.when(i == 0)
    def _():
        for g in heads:
            kn_scr[:, g * dh:(g + 1) * dh] = _rms(
                k_ref[0, :, g * dh:(g + 1) * dh].astype(F32), kw_ref[...]).astype(BF16)

    scale = dh ** -0.5
    qn = [(_rms(q_ref[0, :, g * dh:(g + 1) * dh].astype(F32), qw_ref[...]) * scale).astype(BF16)
          for g in heads]
    row = i * tq + lax.broadcasted_iota(jnp.int32, (tq, kb), 0)
    col0 = lax.broadcasted_iota(jnp.int32, (tq, kb), 1)
    npair = (i + 1) * (tq // (2 * kb))

    def cond(carry):
        p, cs, _ = carry
        cmax = cs[0]
        for g in heads[1:]:
            cmax = jnp.maximum(cmax, cs[g])
        return jnp.logical_and(p < npair, jnp.max(cmax) > SB_LOG_UNDERFLOW)

    def body(carry):
        p, cs, accs = carry
        k0 = pl.multiple_of((npair - 1 - p) * (2 * kb), 2 * kb)
        masks = [(col0 + (k0 + half * kb)) < row for half in range(2)]
        z2 = [_dot_nt(qn[g], kn_scr[pl.ds(k0, 2 * kb), g * dh:(g + 1) * dh]) for g in heads]
        cats, logits = [], []
        for g in heads:
            cat_g, logit_g = [], []
            for half in range(2):
                z = z2[g][:, half * kb:(half + 1) * kb]
                sp = jnp.maximum(z, 0.0) + jnp.log(1.0 + jnp.exp(-jnp.abs(z)))
                lneg = jnp.where(masks[half], -sp, 0.0)
                hi = lneg.astype(BF16)
                lo = (lneg - hi.astype(F32)).astype(BF16)
                cat_g.append(jnp.concatenate([hi, lo], axis=1))
                logit_g.append(z - sp)
            cats.append(jnp.concatenate(cat_g, axis=0))
            logits.append(logit_g)
        res = [_dot(cats[g], tri_ref[...]) for g in heads]
        a2, c_new = [], []
        for g in heads:
            c_early = cs[g] + res[g][tq:, kb:]
            a_late = jnp.where(masks[1], jnp.exp(logits[g][1] + res[g][tq:, :kb] + cs[g]), 0.0)
            a_early = jnp.where(masks[0], jnp.exp(logits[g][0] + res[g][:tq, :kb] + c_early), 0.0)
            a2.append(jnp.concatenate([a_early, a_late], axis=1).astype(BF16))
            c_new.append(c_early + res[g][:tq, kb:])
        accs = tuple(accs[g] + _dot(a2[g], v_ref[0, pl.ds(k0, 2 * kb), g * dh:(g + 1) * dh])
                     for g in heads)
        return p + 1, tuple(c_new), accs

    init = (jnp.int32(0), tuple(jnp.zeros((tq, kb), F32) for _ in heads),
            tuple(jnp.zeros((tq, dh), F32) for _ in heads))
    _, _, accs = lax.while_loop(cond, body, init)
    for g in heads:
        o_ref[0, :, g * dh:(g + 1) * dh] = accs[g].astype(o_ref.dtype)


def _sb_tri():
    r = np.arange(2 * LANES)[:, None] % LANES
    c = np.arange(2 * LANES)[None, :]
    m = np.where(c < LANES, r > c, True)
    return jnp.asarray(m, dtype=BF16)


def _sb_attention(proj3, qw, kw, tq):
    b, s, _ = proj3.shape
    w = SB_GROUP * HEAD_DIM
    ng = HEADS // SB_GROUP
    return pl.pallas_call(
        functools.partial(_sb_kernel, tq=tq),
        out_shape=jax.ShapeDtypeStruct((b, s, HEADS * HEAD_DIM), BF16),
        grid=(b, ng, s // tq),
        in_specs=[
            pl.BlockSpec((1, tq, w), lambda bi, h, i: (bi, i, h)),
            pl.BlockSpec((1, s, w), lambda bi, h, i: (bi, 0, ng + h)),
            pl.BlockSpec((1, s, w), lambda bi, h, i: (bi, 0, 2 * ng + h)),
            pl.BlockSpec((1, HEAD_DIM), lambda bi, h, i: (0, 0)),
            pl.BlockSpec((1, HEAD_DIM), lambda bi, h, i: (0, 0)),
            pl.BlockSpec((2 * LANES, 2 * LANES), lambda bi, h, i: (0, 0)),
        ],
        out_specs=pl.BlockSpec((1, tq, w), lambda bi, h, i: (bi, i, h)),
        scratch_shapes=[pltpu.VMEM((s, w), BF16)],
        compiler_params=pltpu.CompilerParams(
            dimension_semantics=("parallel", "parallel", "arbitrary"),
            vmem_limit_bytes=VMEM_LIMIT),
        name="sb_attn",
    )(proj3, proj3, proj3, qw, kw, _sb_tri())


def _gdn_kernel(alog_ref, dtb_ref, gq_ref, gk_ref, gv_ref, z_ref, cwq_ref, cwk_ref, cwv_ref,
                acol_ref, bcol_ref, arow_ref, onw_ref, o_ref,
                xpad, qs, ks, vs, m_scr, b_scr, qp_scr, op_scr, eg_scr):
    hg = pl.program_id(1)
    s = qs.shape[0]
    c = GDN_CHUNK
    dk = HEAD_DIM
    w = GDN_GROUP * dk
    pad = 8

    def conv_silu(src_ref, cw_ref):
        xpad[pl.ds(0, pad), :] = jnp.zeros((pad, w), F32)
        xpad[pl.ds(pad, s), :] = src_ref[0].astype(F32)
        y = cw_ref[0:1, :] * xpad[pl.ds(pad - 3, s), :]
        for t in range(1, GDN_CONV):
            y = y + cw_ref[t:t + 1, :] * xpad[pl.ds(pad - 3 + t, s), :]
        return y * _sigmoid(y)

    def l2n(x):
        return x * lax.rsqrt(jnp.sum(x * x, axis=-1, keepdims=True) + EPS)

    yq = conv_silu(gq_ref, cwq_ref)
    for g in range(GDN_GROUP):
        qs[:, g * dk:(g + 1) * dk] = l2n(yq[:, g * dk:(g + 1) * dk]) * (dk ** -0.5)
    yk = conv_silu(gk_ref, cwk_ref)
    for g in range(GDN_GROUP):
        ks[:, g * dk:(g + 1) * dk] = l2n(yk[:, g * dk:(g + 1) * dk])
    vs[...] = conv_silu(gv_ref, cwv_ref)

    a_gain = [jnp.exp(jnp.full((1, 1), alog_ref[hg * GDN_GROUP + g], F32)) for g in range(GDN_GROUP)]
    dtb = [jnp.full((1, 1), dtb_ref[hg * GDN_GROUP + g], F32) for g in range(GDN_GROUP)]

    ri = lax.broadcasted_iota(jnp.int32, (c, c), 0)
    ci = lax.broadcasted_iota(jnp.int32, (c, c), 1)
    tril = ri >= ci
    strict = ri > ci
    tril_b = jnp.where(tril, 1.0, 0.0).astype(BF16)
    triu_b = jnp.where(ri <= ci, 1.0, 0.0).astype(BF16)
    eye = jnp.where(ri == ci, 1.0, 0.0)

    def split(x):
        hi = x.astype(BF16)
        return hi, (x - hi.astype(F32)).astype(BF16)

    def prepare(nn):
        pairs = [(g, nn * GDN_UNROLL + j) for j in range(GDN_UNROLL) for g in range(GDN_GROUP)]
        ps = range(len(pairs))
        rows = [pl.ds(pl.multiple_of(n * c, c), c) for _, n in pairs]
        q = [qs[rows[i], g * dk:(g + 1) * dk] for i, (g, _) in enumerate(pairs)]
        k = [ks[rows[i], g * dk:(g + 1) * dk] for i, (g, _) in enumerate(pairs)]
        v = [vs[rows[i], g * dk:(g + 1) * dk] for i, (g, _) in enumerate(pairs)]
        g_col = [-a_gain[g] * _softplus(acol_ref[0, g, n] + dtb[g]) for g, n in pairs]
        g_row = [-a_gain[g] * _softplus(arow_ref[0, g, n] + dtb[g]) for g, n in pairs]
        beta = [jnp.broadcast_to(_sigmoid(bcol_ref[0, g, n]), (c, dk)) for g, n in pairs]
        gcs = [split(jnp.broadcast_to(g_col[i], (c, dk))) for i in ps]
        gc2 = [_dot(tril_b, jnp.concatenate(gcs[i], axis=1)) for i in ps]
        gc = [gc2[i][:, :dk] + gc2[i][:, dk:] for i in ps]
        grs = [jnp.concatenate(split(jnp.broadcast_to(g_row[i], (8, c))), axis=0) for i in ps]
        gr2 = [_dot(grs[i], triu_b) for i in ps]
        gc_row = [gr2[i][0:1] + gr2[i][8:9] for i in ps]
        yield
        decay = [jnp.where(tril, jnp.exp(jnp.where(tril, gc[i][:, :c] - gc_row[i], 0.0)), 0.0)
                 for i in ps]
        kbf = [k[i].astype(BF16) for i in ps]
        k_beta = [k[i] * beta[i] for i in ps]
        u = [-jnp.where(strict, _dot_nt(k_beta[i].astype(BF16), kbf[i]) * decay[i], 0.0) for i in ps]
        mb = [u[i].astype(BF16) for i in ps]
        m2 = [_dot(mb[i], mb[i]) for i in ps]
        yield
        for level in range(5):
            mb = [m2[i].astype(BF16) for i in ps]
            if level < 4:
                both = [_dot(jnp.concatenate([mb[i], u[i].astype(BF16)], axis=0), mb[i]) for i in ps]
                u = [u[i] + m2[i] + both[i][c:] for i in ps]
                m2 = [both[i][:c] for i in ps]
            else:
                u = [u[i] + m2[i] + _dot(u[i].astype(BF16), mb[i]) for i in ps]
            yield
        tb = [(eye + u[i]).astype(BF16) for i in ps]
        egc = [jnp.exp(gc[i]) for i in ps]
        kv_b = [_dot(tb[i], jnp.concatenate([(k_beta[i] * egc[i]).astype(BF16),
                                             (v[i] * beta[i]).astype(BF16)], axis=1)).astype(BF16)
                for i in ps]
        attn_b = [jnp.where(tril, _dot_nt(q[i].astype(BF16), kbf[i]) * decay[i], 0.0).astype(BF16)
                  for i in ps]
        yield
        g_last = [gc[i][c - 1:c, :] for i in ps]
        kw_b = [(k[i] * jnp.exp(g_last[i] - gc[i])).astype(BF16) for i in ps]
        mb_out = [_dot_tn(kw_b[i], kv_b[i]) for i in ps]
        qo_out = [_dot(attn_b[i], kv_b[i]) for i in ps]
        m_out = [mb_out[i][:, :dk].astype(BF16) for i in ps]
        b_out = [mb_out[i][:, dk:] for i in ps]
        qp_out = [(q[i] * egc[i] - qo_out[i][:, :dk]).astype(BF16) for i in ps]
        op_out = [qo_out[i][:, dk:] for i in ps]
        for i, (g, n) in enumerate(pairs):
            m_scr[g, n] = m_out[i]
            b_scr[g, n] = b_out[i]
            qp_scr[g, n] = qp_out[i]
            op_scr[g, n] = op_out[i]
            eg_scr[g, n] = jnp.broadcast_to(jnp.exp(g_last[i]), (8, dk))

    def scan(n, states):
        r0 = pl.multiple_of(n * c, c)
        gs = range(GDN_GROUP)
        sb = [states[g].astype(BF16) for g in gs]
        zc = [z_ref[0, pl.ds(r0, c), g * dk:(g + 1) * dk].astype(F32) for g in gs]
        o = [_dot(qp_scr[g, n], sb[g]) + op_scr[g, n] for g in gs]
        new = tuple(states[g] * eg_scr[g, n][0:1] - _dot(m_scr[g, n], sb[g]) + b_scr[g, n]
                    for g in gs)
        for g in gs:
            o_ref[0, pl.ds(r0, c), g * dk:(g + 1) * dk] = (
                _rms(o[g], onw_ref[...]) * (zc[g] * _sigmoid(zc[g]))).astype(o_ref.dtype)
        return new

    ngroup = s // (c * GDN_UNROLL)
    for _ in prepare(0):
        pass

    def body(nn, states):
        stages = prepare(nn)
        for step in range(GDN_UNROLL):
            next(stages, None)
            states = scan((nn - 1) * GDN_UNROLL + step, states)
        for _ in stages:
            pass
        return states

    states = lax.fori_loop(1, ngroup, body,
                           tuple(jnp.zeros((dk, dk), F32) for _ in range(GDN_GROUP)))
    lax.fori_loop((ngroup - 1) * GDN_UNROLL, ngroup * GDN_UNROLL, scan, states)


def _gdn(proj3, conv_w, a_log, dt_bias, a_col, b_col, a_row, out_norm_w):
    b, s, _ = proj3.shape
    n = s // GDN_CHUNK
    c = GDN_CHUNK
    gg = GDN_GROUP
    w = gg * HEAD_DIM
    assert s % (c * GDN_UNROLL) == 0, s
    col = lambda base: pl.BlockSpec((1, s, w), lambda bi, h: (bi, 0, base // gg + h))
    cw = lambda base: pl.BlockSpec((GDN_CONV, w), lambda bi, h: (0, base // gg + h))
    smem = pl.BlockSpec(memory_space=pltpu.SMEM)
    return pl.pallas_call(
        _gdn_kernel,
        out_shape=jax.ShapeDtypeStruct((b, s, HEADS * HEAD_DIM), BF16),
        grid=(b, HEADS // gg),
        in_specs=[
            smem, smem,
            col(3 * HEADS), col(4 * HEADS), col(5 * HEADS), col(6 * HEADS),
            cw(0), cw(HEADS), cw(2 * HEADS),
            pl.BlockSpec((1, gg, n, c, 1), lambda bi, h: (bi, h, 0, 0, 0)),
            pl.BlockSpec((1, gg, n, c, 1), lambda bi, h: (bi, h, 0, 0, 0)),
            pl.BlockSpec((1, gg, n, 1, c), lambda bi, h: (bi, h, 0, 0, 0)),
            pl.BlockSpec((1, HEAD_DIM), lambda bi, h: (0, 0)),
        ],
        out_specs=pl.BlockSpec((1, s, w), lambda bi, h: (bi, 0, h)),
        scratch_shapes=[
            pltpu.VMEM((s + 8, w), F32), pltpu.VMEM((s, w), F32), pltpu.VMEM((s, w), F32),
            pltpu.VMEM((s, w), F32),
            pltpu.VMEM((gg, n, HEAD_DIM, HEAD_DIM), BF16), pltpu.VMEM((gg, n, HEAD_DIM, HEAD_DIM), F32),
            pltpu.VMEM((gg, n, c, HEAD_DIM), BF16), pltpu.VMEM((gg, n, c, HEAD_DIM), F32),
            pltpu.VMEM((gg, n, 8, HEAD_DIM), F32),
        ],
        compiler_params=pltpu.CompilerParams(
            dimension_semantics=("parallel", "parallel"), vmem_limit_bytes=VMEM_LIMIT),
        name="gdn",
    )(a_log, dt_bias, proj3, proj3, proj3, proj3, conv_w, conv_w, conv_w,
      a_col, b_col, a_row, out_norm_w)


def _merge_kernel(osb_ref, ogdn_ref, gsb_ref, ggdn_ref, x_ref, wsb_ref, wgdn_ref, wout_ref,
                  fnw_ref, wq_ref, k1_ref, k2_ref, x1_ref, h2_ref, s1_ref, s2_ref):
    y_sb = _dot(osb_ref[...], wsb_ref[...])
    y_gdn = _dot(ogdn_ref[...], wgdn_ref[...])
    merged = (_sigmoid(gsb_ref[...].astype(F32)) * y_sb
              + _sigmoid(ggdn_ref[...].astype(F32)) * y_gdn)
    x1 = x_ref[...] + _dot(merged.astype(BF16), wout_ref[...])
    x1_ref[...] = x1
    h2 = _rms(x1, fnw_ref[...]).astype(BF16)
    h2_ref[...] = h2
    q = _dot(h2, wq_ref[...]).astype(BF16)
    for hh in range(PEER_HEADS):
        base = hh * 2 * PEER_HALF
        s1_ref[hh] = _dot_nt(k1_ref[hh], q[:, base:base + PEER_HALF])
        s2_ref[hh] = _dot_nt(k2_ref[hh], q[:, base + PEER_HALF:base + 2 * PEER_HALF])


def _merge(o_sb, o_gdn, proj_main, x2d, w_sb, w_gdn, w_out, ffn_w, w_q, keys1, keys2, tm):
    n = x2d.shape[0]
    full = lambda shape: pl.BlockSpec(shape, lambda i: (0,) * len(shape))
    tok = lambda width, cb=0: pl.BlockSpec((tm, width), lambda i, cb=cb: (i, cb))
    st_spec = pl.BlockSpec((PEER_HEADS, PEER_N_KEYS, tm), lambda i: (0, 0, i))
    return pl.pallas_call(
        _merge_kernel,
        out_shape=(jax.ShapeDtypeStruct((n, D_MODEL), F32),
                   jax.ShapeDtypeStruct((n, D_MODEL), BF16),
                   jax.ShapeDtypeStruct((PEER_HEADS, PEER_N_KEYS, n), F32),
                   jax.ShapeDtypeStruct((PEER_HEADS, PEER_N_KEYS, n), F32)),
        grid=(n // tm,),
        in_specs=[
            tok(D_MODEL), tok(D_MODEL), tok(D_MODEL, 7), tok(D_MODEL, 8), tok(D_MODEL),
            full((D_MODEL, D_MODEL)), full((D_MODEL, D_MODEL)), full((D_MODEL, D_MODEL)),
            full((1, D_MODEL)), full((D_MODEL, 2 * PEER_HALF * PEER_HEADS)),
            full((PEER_HEADS, PEER_N_KEYS, PEER_HALF)), full((PEER_HEADS, PEER_N_KEYS, PEER_HALF)),
        ],
        out_specs=(tok(D_MODEL), tok(D_MODEL), st_spec, st_spec),
        compiler_params=pltpu.CompilerParams(
            dimension_semantics=("parallel",), vmem_limit_bytes=VMEM_LIMIT),
        name="merge",
    )(o_sb, o_gdn, proj_main, proj_main, x2d, w_sb, w_gdn, w_out, ffn_w, w_q, keys1, keys2)


def _top_ranked(s, k):
    assert k <= 16
    big = 2.0 ** 100
    s = jnp.maximum(s, -0.5 * big)
    vals = []
    for r in range(k):
        m = jnp.max(s, axis=0, keepdims=True)
        vals.append(m)
        s = jnp.where(s == m, -big * (1.0 + r / 16.0), s)
    rank = jnp.where(s <= -big, (s * (-1.0 / big) - 1.0) * 16.0, float(k))
    return jnp.concatenate(vals, axis=0), rank


def _topk_kernel(s1_ref, s2_ref, n1_ref, p1_ref, r2_ref, p2_ref, *, tl):
    kk = PEER_TOPK

    def sub(bi, _):
        l0 = pl.multiple_of(bi * LANES, LANES)
        s1 = s1_ref[0, :, pl.ds(l0, LANES)]
        s2 = s2_ref[0, :, pl.ds(l0, LANES)]
        v1, rank1 = _top_ranked(s1, kk)
        v2, rank2 = _top_ranked(s2, kk)
        jrow8 = lax.broadcasted_iota(jnp.int32, (8, LANES), 0)
        blocks = [v1[0:1] + v2]
        for i in range(1, 8):
            blocks.append(v1[i:i + 1] + jnp.where(jrow8 < kk // (i + 1), v2[0:8], NEG_INF))
        blocks.append(v1[8:16] + v2[0:1])
        cand = jnp.concatenate(blocks, axis=0)
        top = cand[0:1]
        work = cand
        tau = top
        for _ in range(kk):
            tau = jnp.max(work, axis=0, keepdims=True)
            work = jnp.where(work == tau, NEG_INF, work)
        sel = cand >= tau
        zsum = jnp.sum(jnp.where(sel, jnp.exp(jnp.where(sel, cand - top, 0.0)), 0.0),
                       axis=0, keepdims=True)
        counts = [jnp.sum(jnp.where(blocks[i] >= tau, 1.0, 0.0), axis=0, keepdims=True)
                  for i in range(8)]
        counts.append(jnp.where(blocks[8] >= tau, 1.0, 0.0))
        nrank = jnp.concatenate(counts, axis=0)
        n1 = jnp.zeros(s1.shape, F32)
        for i in range(kk):
            n1 = jnp.where(rank1 == float(i), nrank[i:i + 1], n1)
        n1_ref[0, :, pl.ds(l0, LANES)] = n1
        p1_ref[0, :, pl.ds(l0, LANES)] = jnp.exp(s1 - v1[0:1]) / zsum
        r2_ref[:, pl.ds(l0, LANES)] = rank2.astype(BF16)
        p2_ref[:, pl.ds(l0, LANES)] = jnp.exp(s2 - v2[0:1]).astype(BF16)
        return 0

    lax.fori_loop(0, tl // LANES, sub, 0)


def _peer_topk(s1t, s2t, tl):
    hh, kk, n = s1t.shape
    spec = pl.BlockSpec((1, kk, tl), lambda i, h: (h, 0, i))
    f32 = jax.ShapeDtypeStruct((hh, kk, n), F32)
    b16 = jax.ShapeDtypeStruct((hh * kk, n), BF16)
    spec2 = pl.BlockSpec((kk, tl), lambda i, h: (h, i))
    return pl.pallas_call(
        functools.partial(_topk_kernel, tl=tl),
        out_shape=(f32, f32, b16, b16),
        grid=(n // tl, hh),
        in_specs=[spec, spec],
        out_specs=(spec, spec, spec2, spec2),
        compiler_params=pltpu.CompilerParams(
            dimension_semantics=("parallel", "parallel"), vmem_limit_bytes=VMEM_LIMIT),
        name="peer_topk",
    )(s1t, s2t)


def _dense_kernel(h2t_ref, u_ref, vt_ref, r2_ref, p2_ref, n1_ref, p1_ref, x1_ref, o_ref,
                  acc_scr, w_scr, r2_scr, p2_scr, *, ec):
    j = pl.program_id(1)
    nk = PEER_N_KEYS
    tl = h2t_ref.shape[1]
    pk = 16

    @pl.when(j == 0)
    def _():
        acc_scr[...] = jnp.zeros_like(acc_scr)
        r2_scr[...] = r2_ref[...]
        p2_scr[...] = p2_ref[...]

    pre = _dot(u_ref[...], h2t_ref[...])
    for e in range(ec // nk):
        e1 = j * (ec // nk) + e
        n1_rows = [jnp.broadcast_to(n1_ref[hh, pl.ds(e1, 1), :], (pk, tl)).astype(BF16)
                   for hh in range(PEER_HEADS)]
        p1_rows = [jnp.broadcast_to(p1_ref[hh, pl.ds(e1, 1), :], (pk, tl)).astype(BF16)
                   for hh in range(PEER_HEADS)]
        for tv in range(tl // LANES):
            lanes = slice(tv * LANES, (tv + 1) * LANES)
            g = None
            for hh in range(PEER_HEADS):
                keys = slice(hh * nk, (hh + 1) * nk)
                n1 = jnp.concatenate([n1_rows[hh][:, lanes]] * (nk // pk), axis=0)
                p1 = jnp.concatenate([p1_rows[hh][:, lanes]] * (nk // pk), axis=0)
                p2 = p2_scr[keys, lanes]
                term = jnp.where(r2_scr[keys, lanes] < n1, p2, jnp.zeros_like(p2)) * p1
                g = term if g is None else g + term
            blk = pre[e * nk:(e + 1) * nk, lanes].astype(BF16)
            act = (0.5 * blk) * (1.0 + lax.erf(blk * (2.0 ** -0.5)))
            w_scr[pl.ds(e * nk, nk), lanes] = act * g
    acc_scr[...] += _dot(vt_ref[...], w_scr[...])

    @pl.when(j == pl.num_programs(1) - 1)
    def _():
        o_ref[...] = x1_ref[...] + acc_scr[...].T


def _peer_dense(h2t, u_tab, vt_tab, r2, p2, n1, p1, x1, tl, ec):
    d, n = h2t.shape
    ne = u_tab.shape[0]
    hspec = pl.BlockSpec((PEER_HEADS, PEER_N_KEYS, tl), lambda i, j: (0, 0, i))
    bspec = pl.BlockSpec((PEER_HEADS * PEER_N_KEYS, tl), lambda i, j: (0, i))
    return pl.pallas_call(
        functools.partial(_dense_kernel, ec=ec),
        out_shape=jax.ShapeDtypeStruct((n, d), F32),
        grid=(n // tl, ne // ec),
        in_specs=[
            pl.BlockSpec((d, tl), lambda i, j: (0, i)),
            pl.BlockSpec((ec, d), lambda i, j: (j, 0)),
            pl.BlockSpec((d, ec), lambda i, j: (0, j)),
            bspec, bspec, hspec, hspec,
            pl.BlockSpec((tl, d), lambda i, j: (i, 0)),
        ],
        out_specs=pl.BlockSpec((tl, d), lambda i, j: (i, 0)),
        scratch_shapes=[pltpu.VMEM((d, tl), F32), pltpu.VMEM((ec, tl), BF16),
                        pltpu.VMEM((PEER_HEADS * PEER_N_KEYS, tl), BF16),
                        pltpu.VMEM((PEER_HEADS * PEER_N_KEYS, tl), BF16)],
        compiler_params=pltpu.CompilerParams(
            dimension_semantics=("parallel", "arbitrary"), vmem_limit_bytes=VMEM_LIMIT),
        name="peer_dense",
    )(h2t, u_tab, vt_tab, r2, p2, n1, p1, x1)


def _tile(n, pref):
    t = min(n, pref)
    assert n % t == 0, (n, t)
    return t


def kernel(x, mix_norm_w, w_in, sb_q_norm_w, sb_k_norm_w, gdn_conv_w, gdn_a_log, gdn_dt_bias,
           gdn_out_norm_w, w_branch_sb, w_branch_gdn, w_out, ffn_norm_w, peer_w_q, peer_keys1,
           peer_keys2, peer_u, peer_v):
    b, s, d = x.shape
    n = b * s
    depth = w_in.shape[0]
    sb_w = HEADS * HEAD_DIM
    small0 = 7 * sb_w
    small1 = small0 + 2 * HEADS
    nchunk = s // GDN_CHUNK
    x2d = x.reshape(n, d)
    for l in range(depth):
        w_main = jnp.concatenate([w_in[l][:, :small0], w_in[l][:, small1:]], axis=1).astype(BF16)
        w_small = jnp.pad(w_in[l][:, small0:small1], ((0, 0), (0, LANES - 2 * HEADS))).astype(BF16)
        proj_main, proj_small = _in_proj(x2d, mix_norm_w[l][None], w_main, w_small,
                                         _tile(n, IN_PROJ_ROWS), IN_PROJ_COLS)
        proj3 = proj_main.reshape(b, s, MAIN_WIDTH)

        o_sb = _sb_attention(proj3, sb_q_norm_w[l][None], sb_k_norm_w[l][None],
                             _tile(s, SB_QUERY_ROWS))

        ba = proj_small[:, :2 * HEADS].reshape(b, nchunk, GDN_CHUNK, 2 * HEADS)
        ba = ba.transpose(0, 3, 1, 2)
        b_col = ba[:, :HEADS, :, :, None]
        a_col = ba[:, HEADS:, :, :, None]
        a_row = ba[:, HEADS:, :, None, :]
        o_gdn = _gdn(proj3, gdn_conv_w[l], gdn_a_log[l], gdn_dt_bias[l], a_col, b_col, a_row,
                     gdn_out_norm_w[l][None])

        x1, h2, s1t, s2t = _merge(
            o_sb.reshape(n, sb_w), o_gdn.reshape(n, sb_w), proj_main, x2d,
            w_branch_sb[l].astype(BF16), w_branch_gdn[l].astype(BF16), w_out[l].astype(BF16),
            ffn_norm_w[l][None], peer_w_q[l].astype(BF16),
            peer_keys1[l].astype(BF16), peer_keys2[l].astype(BF16), _tile(n, MERGE_ROWS))

        n1, p1, r2, p2 = _peer_topk(s1t, s2t, _tile(n, TOPK_TOKENS))
        x2d = _peer_dense(h2.T, peer_u[l].astype(BF16), peer_v[l].astype(BF16).T,
                          r2, p2, n1, p1, x1, _tile(n, DENSE_TOKENS), DENSE_EXPERTS)
    return x2d.reshape(b, s, d)
```

```python
import functools

import jax
import jax.numpy as jnp
import numpy as np
from jax import lax
from jax.experimental import pallas as pl
from jax.experimental.pallas import tpu as pltpu

F32 = jnp.float32
BF16 = jnp.bfloat16

D_MODEL = 1024
HEADS = 8
HEAD_DIM = 128
LANES = 128
GDN_CONV = 4
GDN_CHUNK = 64
PEER_HEADS = 8
PEER_N_KEYS = 128
PEER_HALF = 128
PEER_TOPK = 16
EPS = 1e-6
MAIN_WIDTH = 9 * D_MODEL
VMEM_LIMIT = 56 * 1024 * 1024
SB_LOG_UNDERFLOW = -104.0
SB_GROUP = 8
GDN_GROUP = 2
GDN_UNROLL = 8
IN_PROJ_ROWS, IN_PROJ_COLS = 1024, 4608
SB_QUERY_ROWS = 256
MERGE_ROWS = 512
TOPK_TOKENS = 1024
DENSE_TOKENS, DENSE_EXPERTS = 512, 2048
NEG_INF = float("-inf")
POS_INF = float("inf")


def _sigmoid(x):
    return 1.0 / (1.0 + jnp.exp(-x))


def _softplus(x):
    return jnp.maximum(x, 0.0) + jnp.log1p(jnp.exp(-jnp.abs(x)))


def _rms(x, w):
    return x * lax.rsqrt(jnp.mean(x * x, axis=-1, keepdims=True) + EPS) * w


def _dot(a, b):
    return jnp.dot(a, b, preferred_element_type=F32)


def _dot_nt(a, b):
    return lax.dot_general(a, b, (((1,), (1,)), ((), ())), preferred_element_type=F32)


def _dot_tn(a, b):
    return lax.dot_general(a, b, (((0,), (0,)), ((), ())), preferred_element_type=F32)


def _inproj_kernel(x_ref, nw_ref, w_ref, ws_ref, o_ref, os_ref, h_scr):
    j = pl.program_id(1)

    @pl.when(j == 0)
    def _():
        h = _rms(x_ref[...], nw_ref[...]).astype(BF16)
        h_scr[...] = h
        os_ref[...] = _dot(h, ws_ref[...])

    o_ref[...] = _dot(h_scr[...], w_ref[...]).astype(o_ref.dtype)


def _in_proj(x2d, norm_w, w_main, w_small, tm, tn):
    n = x2d.shape[0]
    return pl.pallas_call(
        _inproj_kernel,
        out_shape=(jax.ShapeDtypeStruct((n, MAIN_WIDTH), BF16),
                   jax.ShapeDtypeStruct((n, LANES), F32)),
        grid=(n // tm, MAIN_WIDTH // tn),
        in_specs=[
            pl.BlockSpec((tm, D_MODEL), lambda i, j: (i, 0)),
            pl.BlockSpec((1, D_MODEL), lambda i, j: (0, 0)),
            pl.BlockSpec((D_MODEL, tn), lambda i, j: (0, j)),
            pl.BlockSpec((D_MODEL, LANES), lambda i, j: (0, 0)),
        ],
        out_specs=(pl.BlockSpec((tm, tn), lambda i, j: (i, j)),
                   pl.BlockSpec((tm, LANES), lambda i, j: (i, 0))),
        scratch_shapes=[pltpu.VMEM((tm, D_MODEL), BF16)],
        compiler_params=pltpu.CompilerParams(
            dimension_semantics=("parallel", "arbitrary"), vmem_limit_bytes=VMEM_LIMIT),
        name="in_proj",
    )(x2d, norm_w, w_main, w_small)


def _sb_kernel(q_ref, k_ref, v_ref, qw_ref, kw_ref, tri_ref, o_ref, kn_scr, *, tq):
    i = pl.program_id(2)
    kb = LANES
    dh = HEAD_DIM
    heads = range(SB_GROUP)

    @pl.when(i == 0)
    def _():
        for g in heads:
            kn_scr[:, g * dh:(g + 1) * dh] = _rms(
                k_ref[0, :, g * dh:(g + 1) * dh].astype(F32), kw_ref[...]).astype(BF16)

    scale = dh ** -0.5
    qn = [(_rms(q_ref[0, :, g * dh:(g + 1) * dh].astype(F32), qw_ref[...]) * scale).astype(BF16)
          for g in heads]
    row = i * tq + lax.broadcasted_iota(jnp.int32, (tq, kb), 0)
    col0 = lax.broadcasted_iota(jnp.int32, (tq, kb), 1)
    npair = (i + 1) * (tq // (2 * kb))

    def cond(carry):
        p, cs, _ = carry
        cmax = cs[0]
        for g in heads[1:]:
            cmax = jnp.maximum(cmax, cs[g])
        return jnp.logical_and(p < npair, jnp.max(cmax) > SB_LOG_UNDERFLOW)

    def body(carry):
        p, cs, accs = carry
        k0 = pl.multiple_of((npair - 1 - p) * (2 * kb), 2 * kb)
        masks = [(col0 + (k0 + half * kb)) < row for half in range(2)]
        z2 = [_dot_nt(qn[g], kn_scr[pl.ds(k0, 2 * kb), g * dh:(g + 1) * dh]) for g in heads]
        cats, logits = [], []
        for g in heads:
            cat_g, logit_g = [], []
            for half in range(2):
                z = z2[g][:, half * kb:(half + 1) * kb]
                sp = jnp.maximum(z, 0.0) + jnp.log(1.0 + jnp.exp(-jnp.abs(z)))
                lneg = jnp.where(masks[half], -sp, 0.0)
                hi = lneg.astype(BF16)
                lo = (lneg - hi.astype(F32)).astype(BF16)
                cat_g.append(jnp.concatenate([hi, lo], axis=1))
                logit_g.append(z - sp)
            cats.append(jnp.concatenate(cat_g, axis=0))
            logits.append(logit_g)
        res = [_dot(cats[g], tri_ref[...]) for g in heads]
        a2, c_new = [], []
        for g in heads:
            c_early = cs[g] + res[g][tq:, kb:]
            a_late = jnp.where(masks[1], jnp.exp(logits[g][1] + res[g][tq:, :kb] + cs[g]), 0.0)
            a_early = jnp.where(masks[0], jnp.exp(logits[g][0] + res[g][:tq, :kb] + c_early), 0.0)
            a2.append(jnp.concatenate([a_early, a_late], axis=1).astype(BF16))
            c_new.append(c_early + res[g][:tq, kb:])
        accs = tuple(accs[g] + _dot(a2[g], v_ref[0, pl.ds(k0, 2 * kb), g * dh:(g + 1) * dh])
                     for g in heads)
        return p + 1, tuple(c_new), accs

    init = (jnp.int32(0), tuple(jnp.zeros((tq, kb), F32) for _ in heads),
            tuple(jnp.zeros((tq, dh), F32) for _ in heads))
    _, _, accs = lax.while_loop(cond, body, init)
    for g in heads:
        o_ref[0, :, g * dh:(g + 1) * dh] = accs[g].astype(o_ref.dtype)


def _sb_tri():
    r = np.arange(2 * LANES)[:, None] % LANES
    c = np.arange(2 * LANES)[None, :]
    m = np.where(c < LANES, r > c, True)
    return jnp.asarray(m, dtype=BF16)


def _sb_attention(proj3, qw, kw, tq):
    b, s, _ = proj3.shape
    w = SB_GROUP * HEAD_DIM
    ng = HEADS // SB_GROUP
    return pl.pallas_call(
        functools.partial(_sb_kernel, tq=tq),
        out_shape=jax.ShapeDtypeStruct((b, s, HEADS * HEAD_DIM), BF16),
        grid=(b, ng, s // tq),
        in_specs=[
            pl.BlockSpec((1, tq, w), lambda bi, h, i: (bi, i, h)),
            pl.BlockSpec((1, s, w), lambda bi, h, i: (bi, 0, ng + h)),
            pl.BlockSpec((1, s, w), lambda bi, h, i: (bi, 0, 2 * ng + h)),
            pl.BlockSpec((1, HEAD_DIM), lambda bi, h, i: (0, 0)),
            pl.BlockSpec((1, HEAD_DIM), lambda bi, h, i: (0, 0)),
            pl.BlockSpec((2 * LANES, 2 * LANES), lambda bi, h, i: (0, 0)),
        ],
        out_specs=pl.BlockSpec((1, tq, w), lambda bi, h, i: (bi, i, h)),
        scratch_shapes=[pltpu.VMEM((s, w), BF16)],
        compiler_params=pltpu.CompilerParams(
            dimension_semantics=("parallel", "parallel", "arbitrary"),
            vmem_limit_bytes=VMEM_LIMIT),
        name="sb_attn",
    )(proj3, proj3, proj3, qw, kw, _sb_tri())


def _gdn_kernel(alog_ref, dtb_ref, gq_ref, gk_ref, gv_ref, z_ref, cwq_ref, cwk_ref, cwv_ref,
                acol_ref, bcol_ref, arow_ref, onw_ref, o_ref,
                xpad, qs, ks, vs, m_scr, b_scr, qp_scr, op_scr, eg_scr):
    hg = pl.program_id(1)
    s = qs.shape[0]
    c = GDN_CHUNK
    dk = HEAD_DIM
    w = GDN_GROUP * dk
    pad = 8

    def conv_silu(src_ref, cw_ref):
        xpad[pl.ds(0, pad), :] = jnp.zeros((pad, w), F32)
        xpad[pl.ds(pad, s), :] = src_ref[0].astype(F32)
        y = cw_ref[0:1, :] * xpad[pl.ds(pad - 3, s), :]
        for t in range(1, GDN_CONV):
            y = y + cw_ref[t:t + 1, :] * xpad[pl.ds(pad - 3 + t, s), :]
        return y * _sigmoid(y)

    def l2n(x):
        return x * lax.rsqrt(jnp.sum(x * x, axis=-1, keepdims=True) + EPS)

    yq = conv_silu(gq_ref, cwq_ref)
    for g in range(GDN_GROUP):
        qs[:, g * dk:(g + 1) * dk] = l2n(yq[:, g * dk:(g + 1) * dk]) * (dk ** -0.5)
    yk = conv_silu(gk_ref, cwk_ref)
    for g in range(GDN_GROUP):
        ks[:, g * dk:(g + 1) * dk] = l2n(yk[:, g * dk:(g + 1) * dk])
    vs[...] = conv_silu(gv_ref, cwv_ref)

    a_gain = [jnp.exp(jnp.full((1, 1), alog_ref[hg * GDN_GROUP + g], F32)) for g in range(GDN_GROUP)]
    dtb = [jnp.full((1, 1), dtb_ref[hg * GDN_GROUP + g], F32) for g in range(GDN_GROUP)]

    ri = lax.broadcasted_iota(jnp.int32, (c, c), 0)
    ci = lax.broadcasted_iota(jnp.int32, (c, c), 1)
    tril = ri >= ci
    strict = ri > ci
    tril_b = jnp.where(tril, 1.0, 0.0).astype(BF16)
    triu_b = jnp.where(ri <= ci, 1.0, 0.0).astype(BF16)
    eye = jnp.where(ri == ci, 1.0, 0.0)

    def split(x):
        hi = x.astype(BF16)
        return hi, (x - hi.astype(F32)).astype(BF16)

    def prepare(nn):
        pairs = [(g, nn * GDN_UNROLL + j) for j in range(GDN_UNROLL) for g in range(GDN_GROUP)]
        ps = range(len(pairs))
        rows = [pl.ds(pl.multiple_of(n * c, c), c) for _, n in pairs]
        q = [qs[rows[i], g * dk:(g + 1) * dk] for i, (g, _) in enumerate(pairs)]
        k = [ks[rows[i], g * dk:(g + 1) * dk] for i, (g, _) in enumerate(pairs)]
        v = [vs[rows[i], g * dk:(g + 1) * dk] for i, (g, _) in enumerate(pairs)]
        g_col = [-a_gain[g] * _softplus(acol_ref[0, g, n] + dtb[g]) for g, n in pairs]
        g_row = [-a_gain[g] * _softplus(arow_ref[0, g, n] + dtb[g]) for g, n in pairs]
        beta = [jnp.broadcast_to(_sigmoid(bcol_ref[0, g, n]), (c, dk)) for g, n in pairs]
        gcs = [split(jnp.broadcast_to(g_col[i], (c, dk))) for i in ps]
        gc2 = [_dot(tril_b, jnp.concatenate(gcs[i], axis=1)) for i in ps]
        gc = [gc2[i][:, :dk] + gc2[i][:, dk:] for i in ps]
        grs = [jnp.concatenate(split(jnp.broadcast_to(g_row[i], (8, c))), axis=0) for i in ps]
        gr2 = [_dot(grs[i], triu_b) for i in ps]
        gc_row = [gr2[i][0:1] + gr2[i][8:9] for i in ps]
        yield
        decay = [jnp.where(tril, jnp.exp(jnp.where(tril, gc[i][:, :c] - gc_row[i], 0.0)), 0.0)
                 for i in ps]
        kbf = [k[i].astype(BF16) for i in ps]
        k_beta = [k[i] * beta[i] for i in ps]
        u = [-jnp.where(strict, _dot_nt(k_beta[i].astype(BF16), kbf[i]) * decay[i], 0.0) for i in ps]
        mb = [u[i].astype(BF16) for i in ps]
        m2 = [_dot(mb[i], mb[i]) for i in ps]
        yield
        for level in range(5):
            mb = [m2[i].astype(BF16) for i in ps]
            if level < 4:
                both = [_dot(jnp.concatenate([mb[i], u[i].astype(BF16)], axis=0), mb[i]) for i in ps]
                u = [u[i] + m2[i] + both[i][c:] for i in ps]
                m2 = [both[i][:c] for i in ps]
            else:
                u = [u[i] + m2[i] + _dot(u[i].astype(BF16), mb[i]) for i in ps]
            yield
        tb = [(eye + u[i]).astype(BF16) for i in ps]
        egc = [jnp.exp(gc[i]) for i in ps]
        kv_b = [_dot(tb[i], jnp.concatenate([(k_beta[i] * egc[i]).astype(BF16),
                                             (v[i] * beta[i]).astype(BF16)], axis=1)).astype(BF16)
                for i in ps]
        attn_b = [jnp.where(tril, _dot_nt(q[i].astype(BF16), kbf[i]) * decay[i], 0.0).astype(BF16)
                  for i in ps]
        yield
        g_last = [gc[i][c - 1:c, :] for i in ps]
        kw_b = [(k[i] * jnp.exp(g_last[i] - gc[i])).astype(BF16) for i in ps]
        mb_out = [_dot_tn(kw_b[i], kv_b[i]) for i in ps]
        qo_out = [_dot(attn_b[i], kv_b[i]) for i in ps]
        m_out = [mb_out[i][:, :dk].astype(BF16) for i in ps]
        b_out = [mb_out[i][:, dk:] for i in ps]
        qp_out = [(q[i] * egc[i] - qo_out[i][:, :dk]).astype(BF16) for i in ps]
        op_out = [qo_out[i][:, dk:] for i in ps]
        for i, (g, n) in enumerate(pairs):
            m_scr[g, n] = m_out[i]
            b_scr[g, n] = b_out[i]
            qp_scr[g, n] = qp_out[i]
            op_scr[g, n] = op_out[i]
            eg_scr[g, n] = jnp.broadcast_to(jnp.exp(g_last[i]), (8, dk))

    def scan(n, states):
        r0 = pl.multiple_of(n * c, c)
        gs = range(GDN_GROUP)
        sb = [states[g].astype(BF16) for g in gs]
        zc = [z_ref[0, pl.ds(r0, c), g * dk:(g + 1) * dk].astype(F32) for g in gs]
        o = [_dot(qp_scr[g, n], sb[g]) + op_scr[g, n] for g in gs]
        new = tuple(states[g] * eg_scr[g, n][0:1] - _dot(m_scr[g, n], sb[g]) + b_scr[g, n]
                    for g in gs)
        for g in gs:
            o_ref[0, pl.ds(r0, c), g * dk:(g + 1) * dk] = (
                _rms(o[g], onw_ref[...]) * (zc[g] * _sigmoid(zc[g]))).astype(o_ref.dtype)
        return new

    ngroup = s // (c * GDN_UNROLL)
    for _ in prepare(0):
        pass

    def body(nn, states):
        stages = prepare(nn)
        for step in range(GDN_UNROLL):
            next(stages, None)
            states = scan((nn - 1) * GDN_UNROLL + step, states)
        for _ in stages:
            pass
        return states

    states = lax.fori_loop(1, ngroup, body,
                           tuple(jnp.zeros((dk, dk), F32) for _ in range(GDN_GROUP)))
    lax.fori_loop((ngroup - 1) * GDN_UNROLL, ngroup * GDN_UNROLL, scan, states)


def _gdn(proj3, conv_w, a_log, dt_bias, a_col, b_col, a_row, out_norm_w):
    b, s, _ = proj3.shape
    n = s // GDN_CHUNK
    c = GDN_CHUNK
    gg = GDN_GROUP
    w = gg * HEAD_DIM
    assert s % (c * GDN_UNROLL) == 0, s
    col = lambda base: pl.BlockSpec((1, s, w), lambda bi, h: (bi, 0, base // gg + h))
    cw = lambda base: pl.BlockSpec((GDN_CONV, w), lambda bi, h: (0, base // gg + h))
    smem = pl.BlockSpec(memory_space=pltpu.SMEM)
    return pl.pallas_call(
        _gdn_kernel,
        out_shape=jax.ShapeDtypeStruct((b, s, HEADS * HEAD_DIM), BF16),
        grid=(b, HEADS // gg),
        in_specs=[
            smem, smem,
            col(3 * HEADS), col(4 * HEADS), col(5 * HEADS), col(6 * HEADS),
            cw(0), cw(HEADS), cw(2 * HEADS),
            pl.BlockSpec((1, gg, n, c, 1), lambda bi, h: (bi, h, 0, 0, 0)),
            pl.BlockSpec((1, gg, n, c, 1), lambda bi, h: (bi, h, 0, 0, 0)),
            pl.BlockSpec((1, gg, n, 1, c), lambda bi, h: (bi, h, 0, 0, 0)),
            pl.BlockSpec((1, HEAD_DIM), lambda bi, h: (0, 0)),
        ],
        out_specs=pl.BlockSpec((1, s, w), lambda bi, h: (bi, 0, h)),
        scratch_shapes=[
            pltpu.VMEM((s + 8, w), F32), pltpu.VMEM((s, w), F32), pltpu.VMEM((s, w), F32),
            pltpu.VMEM((s, w), F32),
            pltpu.VMEM((gg, n, HEAD_DIM, HEAD_DIM), BF16), pltpu.VMEM((gg, n, HEAD_DIM, HEAD_DIM), F32),
            pltpu.VMEM((gg, n, c, HEAD_DIM), BF16), pltpu.VMEM((gg, n, c, HEAD_DIM), F32),
            pltpu.VMEM((gg, n, 8, HEAD_DIM), F32),
        ],
        compiler_params=pltpu.CompilerParams(
            dimension_semantics=("parallel", "parallel"), vmem_limit_bytes=VMEM_LIMIT),
        name="gdn",
    )(a_log, dt_bias, proj3, proj3, proj3, proj3, conv_w, conv_w, conv_w,
      a_col, b_col, a_row, out_norm_w)


def _merge_kernel(osb_ref, ogdn_ref, gsb_ref, ggdn_ref, x_ref, wsb_ref, wgdn_ref, wout_ref,
                  fnw_ref, wq_ref, k1_ref, k2_ref, x1_ref, h2t_ref, s1_ref, s2_ref):
    y_sb = _dot(osb_ref[...], wsb_ref[...])
    y_gdn = _dot(ogdn_ref[...], wgdn_ref[...])
    merged = (_sigmoid(gsb_ref[...].astype(F32)) * y_sb
              + _sigmoid(ggdn_ref[...].astype(F32)) * y_gdn)
    x1 = x_ref[...] + _dot(merged.astype(BF16), wout_ref[...])
    x1_ref[...] = x1
    h2f = _rms(x1, fnw_ref[...])
    h2 = h2f.astype(BF16)
    h2t_ref[...] = h2f.T.astype(BF16)
    q = _dot(h2, wq_ref[...]).astype(BF16)
    for hh in range(PEER_HEADS):
        base = hh * 2 * PEER_HALF
        s1_ref[hh] = _dot_nt(k1_ref[hh], q[:, base:base + PEER_HALF])
        s2_ref[hh] = _dot_nt(k2_ref[hh], q[:, base + PEER_HALF:base + 2 * PEER_HALF])


def _merge(o_sb, o_gdn, proj_main, x2d, w_sb, w_gdn, w_out, ffn_w, w_q, keys1, keys2, tm):
    n = x2d.shape[0]
    full = lambda shape: pl.BlockSpec(shape, lambda i: (0,) * len(shape))
    tok = lambda width, cb=0: pl.BlockSpec((tm, width), lambda i, cb=cb: (i, cb))
    st_spec = pl.BlockSpec((PEER_HEADS, PEER_N_KEYS, tm), lambda i: (0, 0, i))
    return pl.pallas_call(
        _merge_kernel,
        out_shape=(jax.ShapeDtypeStruct((n, D_MODEL), F32),
                   jax.ShapeDtypeStruct((D_MODEL, n), BF16),
                   jax.ShapeDtypeStruct((PEER_HEADS, PEER_N_KEYS, n), F32),
                   jax.ShapeDtypeStruct((PEER_HEADS, PEER_N_KEYS, n), F32)),
        grid=(n // tm,),
        in_specs=[
            tok(D_MODEL), tok(D_MODEL), tok(D_MODEL, 7), tok(D_MODEL, 8), tok(D_MODEL),
            full((D_MODEL, D_MODEL)), full((D_MODEL, D_MODEL)), full((D_MODEL, D_MODEL)),
            full((1, D_MODEL)), full((D_MODEL, 2 * PEER_HALF * PEER_HEADS)),
            full((PEER_HEADS, PEER_N_KEYS, PEER_HALF)), full((PEER_HEADS, PEER_N_KEYS, PEER_HALF)),
        ],
        out_specs=(tok(D_MODEL), pl.BlockSpec((D_MODEL, tm), lambda i: (0, i)), st_spec, st_spec),
        compiler_params=pltpu.CompilerParams(
            dimension_semantics=("parallel",), vmem_limit_bytes=VMEM_LIMIT),
        name="merge",
    )(o_sb, o_gdn, proj_main, proj_main, x2d, w_sb, w_gdn, w_out, ffn_w, w_q, keys1, keys2)


def _top_ranked(s, k):
    assert k <= 16
    big = 2.0 ** 100
    s = jnp.maximum(s, -0.5 * big)
    vals = []
    for r in range(k):
        m = jnp.max(s, axis=0, keepdims=True)
        vals.append(m)
        s = jnp.where(s == m, -big * (1.0 + r / 16.0), s)
    rank = jnp.where(s <= -big, (s * (-1.0 / big) - 1.0) * 16.0, float(k))
    return jnp.concatenate(vals, axis=0), rank


def _topk_kernel(s1_ref, s2_ref, n1_ref, p1_ref, r2_ref, p2_ref, *, tl):
    kk = PEER_TOPK

    def sub(bi, _):
        l0 = pl.multiple_of(bi * LANES, LANES)
        s1 = s1_ref[0, :, pl.ds(l0, LANES)]
        s2 = s2_ref[0, :, pl.ds(l0, LANES)]
        v1, rank1 = _top_ranked(s1, kk)
        v2, rank2 = _top_ranked(s2, kk)
        jrow8 = lax.broadcasted_iota(jnp.int32, (8, LANES), 0)
        blocks = [v1[0:1] + v2]
        for i in range(1, 8):
            blocks.append(v1[i:i + 1] + jnp.where(jrow8 < kk // (i + 1), v2[0:8], NEG_INF))
        blocks.append(v1[8:16] + v2[0:1])
        cand = jnp.concatenate(blocks, axis=0)
        top = cand[0:1]
        work = cand
        tau = top
        for _ in range(kk):
            tau = jnp.max(work, axis=0, keepdims=True)
            work = jnp.where(work == tau, NEG_INF, work)
        sel = cand >= tau
        zsum = jnp.sum(jnp.where(sel, jnp.exp(jnp.where(sel, cand - top, 0.0)), 0.0),
                       axis=0, keepdims=True)
        counts = [jnp.sum(jnp.where(blocks[i] >= tau, 1.0, 0.0), axis=0, keepdims=True)
                  for i in range(8)]
        counts.append(jnp.where(blocks[8] >= tau, 1.0, 0.0))
        nrank = jnp.concatenate(counts, axis=0)
        n1 = jnp.zeros(s1.shape, F32)
        for i in range(kk):
            n1 = jnp.where(rank1 == float(i), nrank[i:i + 1], n1)
        n1_ref[0, :, pl.ds(l0, LANES)] = n1
        p1_ref[0, :, pl.ds(l0, LANES)] = jnp.exp(s1 - v1[0:1]) / zsum
        r2_ref[:, pl.ds(l0, LANES)] = rank2.astype(BF16)
        p2_ref[:, pl.ds(l0, LANES)] = jnp.exp(s2 - v2[0:1]).astype(BF16)
        return 0

    lax.fori_loop(0, tl // LANES, sub, 0)


def _peer_topk(s1t, s2t, tl):
    hh, kk, n = s1t.shape
    spec = pl.BlockSpec((1, kk, tl), lambda i, h: (h, 0, i))
    f32 = jax.ShapeDtypeStruct((hh, kk, n), F32)
    b16 = jax.ShapeDtypeStruct((hh * kk, n), BF16)
    spec2 = pl.BlockSpec((kk, tl), lambda i, h: (h, i))
    return pl.pallas_call(
        functools.partial(_topk_kernel, tl=tl),
        out_shape=(f32, f32, b16, b16),
        grid=(n // tl, hh),
        in_specs=[spec, spec],
        out_specs=(spec, spec, spec2, spec2),
        compiler_params=pltpu.CompilerParams(
            dimension_semantics=("parallel", "parallel"), vmem_limit_bytes=VMEM_LIMIT),
        name="peer_topk",
    )(s1t, s2t)


def _dense_kernel(h2t_ref, u_ref, vt_ref, r2_ref, p2_ref, n1_ref, p1_ref, x1_ref, o_ref,
                  acc_scr, w_scr, r2_scr, p2_scr, *, ec):
    j = pl.program_id(1)
    nk = PEER_N_KEYS
    tl = h2t_ref.shape[1]
    pk = 16

    @pl.when(j == 0)
    def _():
        acc_scr[...] = jnp.zeros_like(acc_scr)
        r2_scr[...] = r2_ref[...]
        p2_scr[...] = p2_ref[...]

    pre = _dot(u_ref[...], h2t_ref[...])
    for e in range(ec // nk):
        e1 = j * (ec // nk) + e
        n1_rows = [jnp.broadcast_to(n1_ref[hh, pl.ds(e1, 1), :], (pk, tl)).astype(BF16)
                   for hh in range(PEER_HEADS)]
        p1_rows = [jnp.broadcast_to(p1_ref[hh, pl.ds(e1, 1), :], (pk, tl)).astype(BF16)
                   for hh in range(PEER_HEADS)]
        for tv in range(tl // LANES):
            lanes = slice(tv * LANES, (tv + 1) * LANES)
            g = None
            for hh in range(PEER_HEADS):
                keys = slice(hh * nk, (hh + 1) * nk)
                n1 = jnp.concatenate([n1_rows[hh][:, lanes]] * (nk // pk), axis=0)
                p1 = jnp.concatenate([p1_rows[hh][:, lanes]] * (nk // pk), axis=0)
                p2 = p2_scr[keys, lanes]
                term = jnp.where(r2_scr[keys, lanes] < n1, p2, jnp.zeros_like(p2)) * p1
                g = term if g is None else g + term
            blk = pre[e * nk:(e + 1) * nk, lanes].astype(BF16)
            act = (0.5 * blk) * (1.0 + lax.erf(blk * (2.0 ** -0.5)))
            w_scr[pl.ds(e * nk, nk), lanes] = act * g
    acc_scr[...] += _dot(vt_ref[...], w_scr[...])

    @pl.when(j == pl.num_programs(1) - 1)
    def _():
        o_ref[...] = x1_ref[...] + acc_scr[...].T


def _peer_dense(h2t, u_tab, vt_tab, r2, p2, n1, p1, x1, tl, ec):
    d, n = h2t.shape
    ne = u_tab.shape[0]
    hspec = pl.BlockSpec((PEER_HEADS, PEER_N_KEYS, tl), lambda i, j: (0, 0, i))
    bspec = pl.BlockSpec((PEER_HEADS * PEER_N_KEYS, tl), lambda i, j: (0, i))
    return pl.pallas_call(
        functools.partial(_dense_kernel, ec=ec),
        out_shape=jax.ShapeDtypeStruct((n, d), F32),
        grid=(n // tl, ne // ec),
        in_specs=[
            pl.BlockSpec((d, tl), lambda i, j: (0, i)),
            pl.BlockSpec((ec, d), lambda i, j: (j, 0)),
            pl.BlockSpec((d, ec), lambda i, j: (0, j)),
            bspec, bspec, hspec, hspec,
            pl.BlockSpec((tl, d), lambda i, j: (i, 0)),
        ],
        out_specs=pl.BlockSpec((tl, d), lambda i, j: (i, 0)),
        scratch_shapes=[pltpu.VMEM((d, tl), F32), pltpu.VMEM((ec, tl), BF16),
                        pltpu.VMEM((PEER_HEADS * PEER_N_KEYS, tl), BF16),
                        pltpu.VMEM((PEER_HEADS * PEER_N_KEYS, tl), BF16)],
        compiler_params=pltpu.CompilerParams(
            dimension_semantics=("parallel", "arbitrary"), vmem_limit_bytes=VMEM_LIMIT),
        name="peer_dense",
    )(h2t, u_tab, vt_tab, r2, p2, n1, p1, x1)


def _tile(n, pref):
    t = min(n, pref)
    assert n % t == 0, (n, t)
    return t


def kernel(x, mix_norm_w, w_in, sb_q_norm_w, sb_k_norm_w, gdn_conv_w, gdn_a_log, gdn_dt_bias,
           gdn_out_norm_w, w_branch_sb, w_branch_gdn, w_out, ffn_norm_w, peer_w_q, peer_keys1,
           peer_keys2, peer_u, peer_v):
    b, s, d = x.shape
    n = b * s
    depth = w_in.shape[0]
    sb_w = HEADS * HEAD_DIM
    small0 = 7 * sb_w
    small1 = small0 + 2 * HEADS
    nchunk = s // GDN_CHUNK
    x2d = x.reshape(n, d)
    for l in range(depth):
        w_main = jnp.concatenate([w_in[l][:, :small0], w_in[l][:, small1:]], axis=1).astype(BF16)
        w_small = jnp.pad(w_in[l][:, small0:small1], ((0, 0), (0, LANES - 2 * HEADS))).astype(BF16)
        proj_main, proj_small = _in_proj(x2d, mix_norm_w[l][None], w_main, w_small,
                                         _tile(n, IN_PROJ_ROWS), IN_PROJ_COLS)
        proj3 = proj_main.reshape(b, s, MAIN_WIDTH)

        o_sb = _sb_attention(proj3, sb_q_norm_w[l][None], sb_k_norm_w[l][None],
                             _tile(s, SB_QUERY_ROWS))

        ba = proj_small[:, :2 * HEADS].reshape(b, nchunk, GDN_CHUNK, 2 * HEADS)
        ba = ba.transpose(0, 3, 1, 2)
        b_col = ba[:, :HEADS, :, :, None]
        a_col = ba[:, HEADS:, :, :, None]
        a_row = ba[:, HEADS:, :, None, :]
        o_gdn = _gdn(proj3, gdn_conv_w[l], gdn_a_log[l], gdn_dt_bias[l], a_col, b_col, a_row,
                     gdn_out_norm_w[l][None])

        x1, h2t, s1t, s2t = _merge(
            o_sb.reshape(n, sb_w), o_gdn.reshape(n, sb_w), proj_main, x2d,
            w_branch_sb[l].astype(BF16), w_branch_gdn[l].astype(BF16), w_out[l].astype(BF16),
            ffn_norm_w[l][None], peer_w_q[l].astype(BF16),
            peer_keys1[l].astype(BF16), peer_keys2[l].astype(BF16), _tile(n, MERGE_ROWS))

        n1, p1, r2, p2 = _peer_topk(s1t, s2t, _tile(n, TOPK_TOKENS))
        x2d = _peer_dense(h2t, peer_u[l].astype(BF16), peer_v[l].astype(BF16).T,
                          r2, p2, n1, p1, x1, _tile(n, DENSE_TOKENS), DENSE_EXPERTS)
    return x2d.reshape(b, s, d)
```

```python
import functools

import jax
import jax.numpy as jnp
import numpy as np
from jax import lax
from jax.experimental import pallas as pl
from jax.experimental.pallas import tpu as pltpu

F32 = jnp.float32
BF16 = jnp.bfloat16

D_MODEL = 1024
HEADS = 8
HEAD_DIM = 128
LANES = 128
GDN_CONV = 4
GDN_CHUNK = 64
PEER_HEADS = 8
PEER_N_KEYS = 128
PEER_HALF = 128
PEER_TOPK = 16
EPS = 1e-6
MAIN_WIDTH = 9 * D_MODEL
VMEM_LIMIT = 56 * 1024 * 1024
SB_LOG_UNDERFLOW = -104.0
SB_GROUP = 8
GDN_GROUP = 2
GDN_UNROLL = 8
IN_PROJ_ROWS, IN_PROJ_COLS = 1024, 4608
SB_QUERY_ROWS = 256
MERGE_ROWS = 512
TOPK_TOKENS = 1024
DENSE_TOKENS, DENSE_EXPERTS = 512, 2048
NEG_INF = float("-inf")
POS_INF = float("inf")


def _sigmoid(x):
    return 1.0 / (1.0 + jnp.exp(-x))


def _softplus(x):
    return jnp.maximum(x, 0.0) + jnp.log1p(jnp.exp(-jnp.abs(x)))


def _rms(x, w):
    return x * lax.rsqrt(jnp.mean(x * x, axis=-1, keepdims=True) + EPS) * w


def _dot(a, b):
    return jnp.dot(a, b, preferred_element_type=F32)


def _dot_nt(a, b):
    return lax.dot_general(a, b, (((1,), (1,)), ((), ())), preferred_element_type=F32)


def _dot_tn(a, b):
    return lax.dot_general(a, b, (((0,), (0,)), ((), ())), preferred_element_type=F32)


def _inproj_kernel(x_ref, nw_ref, w_ref, ws_ref, o_ref, os_ref, h_scr):
    j = pl.program_id(1)

    @pl.when(j == 0)
    def _():
        h = _rms(x_ref[...], nw_ref[...]).astype(BF16)
        h_scr[...] = h
        os_ref[...] = _dot(h, ws_ref[...])

    o_ref[...] = _dot(h_scr[...], w_ref[...]).astype(o_ref.dtype)


def _in_proj(x2d, norm_w, w_main, w_small, tm, tn):
    n = x2d.shape[0]
    return pl.pallas_call(
        _inproj_kernel,
        out_shape=(jax.ShapeDtypeStruct((n, MAIN_WIDTH), BF16),
                   jax.ShapeDtypeStruct((n, LANES), F32)),
        grid=(n // tm, MAIN_WIDTH // tn),
        in_specs=[
            pl.BlockSpec((tm, D_MODEL), lambda i, j: (i, 0)),
            pl.BlockSpec((1, D_MODEL), lambda i, j: (0, 0)),
            pl.BlockSpec((D_MODEL, tn), lambda i, j: (0, j)),
            pl.BlockSpec((D_MODEL, LANES), lambda i, j: (0, 0)),
        ],
        out_specs=(pl.BlockSpec((tm, tn), lambda i, j: (i, j)),
                   pl.BlockSpec((tm, LANES), lambda i, j: (i, 0))),
        scratch_shapes=[pltpu.VMEM((tm, D_MODEL), BF16)],
        compiler_params=pltpu.CompilerParams(
            dimension_semantics=("parallel", "arbitrary"), vmem_limit_bytes=VMEM_LIMIT),
        name="in_proj",
    )(x2d, norm_w, w_main, w_small)


def _sb_kernel(q_ref, k_ref, v_ref, qw_ref, kw_ref, tri_ref, o_ref, kn_scr, *, tq):
    i = pl.program_id(2)
    kb = LANES
    dh = HEAD_DIM
    heads = range(SB_GROUP)

    @pl.when(i == 0)
    def _():
        for g in heads:
            kn_scr[:, g * dh:(g + 1) * dh] = _rms(
                k_ref[0, :, g * dh:(g + 1) * dh].astype(F32), kw_ref[...]).astype(BF16)

    scale = dh ** -0.5
    qn = [(_rms(q_ref[0, :, g * dh:(g + 1) * dh].astype(F32), qw_ref[...]) * scale).astype(BF16)
          for g in heads]
    row = i * tq + lax.broadcasted_iota(jnp.int32, (tq, kb), 0)
    col0 = lax.broadcasted_iota(jnp.int32, (tq, kb), 1)
    npair = (i + 1) * (tq // (2 * kb))

    def cond(carry):
        p, cs, _ = carry
        cmax = cs[0]
        for g in heads[1:]:
            cmax = jnp.maximum(cmax, cs[g])
        return jnp.logical_and(p < npair, jnp.max(cmax) > SB_LOG_UNDERFLOW)

    def body(carry):
        p, cs, accs = carry
        k0 = pl.multiple_of((npair - 1 - p) * (2 * kb), 2 * kb)
        masks = [(col0 + (k0 + half * kb)) < row for half in range(2)]
        z2 = [_dot_nt(qn[g], kn_scr[pl.ds(k0, 2 * kb), g * dh:(g + 1) * dh]) for g in heads]
        cats, logits = [], []
        for g in heads:
            cat_g, logit_g = [], []
            for half in range(2):
                z = z2[g][:, half * kb:(half + 1) * kb]
                sp = jnp.maximum(z, 0.0) + jnp.log(1.0 + jnp.exp(-jnp.abs(z)))
                lneg = jnp.where(masks[half], -sp, 0.0)
                hi = lneg.astype(BF16)
                lo = (lneg - hi.astype(F32)).astype(BF16)
                cat_g.append(jnp.concatenate([hi, lo], axis=1))
                logit_g.append(z - sp)
            cats.append(jnp.concatenate(cat_g, axis=0))
            logits.append(logit_g)
        res = [_dot(cats[g], tri_ref[...]) for g in heads]
        a2, c_new = [], []
        for g in heads:
            c_early = cs[g] + res[g][tq:, kb:]
            a_late = jnp.where(masks[1], jnp.exp(logits[g][1] + res[g][tq:, :kb] + cs[g]), 0.0)
            a_early = jnp.where(masks[0], jnp.exp(logits[g][0] + res[g][:tq, :kb] + c_early), 0.0)
            a2.append(jnp.concatenate([a_early, a_late], axis=1).astype(BF16))
            c_new.append(c_early + res[g][:tq, kb:])
        accs = tuple(accs[g] + _dot(a2[g], v_ref[0, pl.ds(k0, 2 * kb), g * dh:(g + 1) * dh])
                     for g in heads)
        return p + 1, tuple(c_new), accs

    init = (jnp.int32(0), tuple(jnp.zeros((tq, kb), F32) for _ in heads),
            tuple(jnp.zeros((tq, dh), F32) for _ in heads))
    _, _, accs = lax.while_loop(cond, body, init)
    for g in heads:
        o_ref[0, :, g * dh:(g + 1) * dh] = accs[g].astype(o_ref.dtype)


def _sb_tri():
    r = np.arange(2 * LANES)[:, None] % LANES
    c = np.arange(2 * LANES)[None, :]
    m = np.where(c < LANES, r > c, True)
    return jnp.asarray(m, dtype=BF16)


def _sb_attention(proj3, qw, kw, tq):
    b, s, _ = proj3.shape
    w = SB_GROUP * HEAD_DIM
    ng = HEADS // SB_GROUP
    return pl.pallas_call(
        functools.partial(_sb_kernel, tq=tq),
        out_shape=jax.ShapeDtypeStruct((b, s, HEADS * HEAD_DIM), BF16),
        grid=(b, ng, s // tq),
        in_specs=[
            pl.BlockSpec((1, tq, w), lambda bi, h, i: (bi, i, h)),
            pl.BlockSpec((1, s, w), lambda bi, h, i: (bi, 0, ng + h)),
            pl.BlockSpec((1, s, w), lambda bi, h, i: (bi, 0, 2 * ng + h)),
            pl.BlockSpec((1, HEAD_DIM), lambda bi, h, i: (0, 0)),
            pl.BlockSpec((1, HEAD_DIM), lambda bi, h, i: (0, 0)),
            pl.BlockSpec((2 * LANES, 2 * LANES), lambda bi, h, i: (0, 0)),
        ],
        out_specs=pl.BlockSpec((1, tq, w), lambda bi, h, i: (bi, i, h)),
        scratch_shapes=[pltpu.VMEM((s, w), BF16)],
        compiler_params=pltpu.CompilerParams(
            dimension_semantics=("parallel", "parallel", "arbitrary"),
            vmem_limit_bytes=VMEM_LIMIT),
        name="sb_attn",
    )(proj3, proj3, proj3, qw, kw, _sb_tri())


def _gdn_kernel(alog_ref, dtb_ref, gq_ref, gk_ref, gv_ref, z_ref, cwq_ref, cwk_ref, cwv_ref,
                a_ref, b_ref, onw_ref, o_ref,
                xpad, qs, ks, vs, m_scr, b_scr, qp_scr, op_scr, eg_scr):
    hg = pl.program_id(1)
    s = qs.shape[0]
    c = GDN_CHUNK
    dk = HEAD_DIM
    w = GDN_GROUP * dk
    pad = 8

    def conv_silu(src_ref, cw_ref):
        xpad[pl.ds(0, pad), :] = jnp.zeros((pad, w), F32)
        xpad[pl.ds(pad, s), :] = src_ref[0].astype(F32)
        y = cw_ref[0:1, :] * xpad[pl.ds(pad - 3, s), :]
        for t in range(1, GDN_CONV):
            y = y + cw_ref[t:t + 1, :] * xpad[pl.ds(pad - 3 + t, s), :]
        return y * _sigmoid(y)

    def l2n(x):
        return x * lax.rsqrt(jnp.sum(x * x, axis=-1, keepdims=True) + EPS)

    yq = conv_silu(gq_ref, cwq_ref)
    for g in range(GDN_GROUP):
        qs[:, g * dk:(g + 1) * dk] = l2n(yq[:, g * dk:(g + 1) * dk]) * (dk ** -0.5)
    yk = conv_silu(gk_ref, cwk_ref)
    for g in range(GDN_GROUP):
        ks[:, g * dk:(g + 1) * dk] = l2n(yk[:, g * dk:(g + 1) * dk])
    vs[...] = conv_silu(gv_ref, cwv_ref)

    a_gain = [jnp.exp(jnp.full((1, 1), alog_ref[hg * GDN_GROUP + g], F32)) for g in range(GDN_GROUP)]
    dtb = [jnp.full((1, 1), dtb_ref[hg * GDN_GROUP + g], F32) for g in range(GDN_GROUP)]

    ri = lax.broadcasted_iota(jnp.int32, (c, c), 0)
    ci = lax.broadcasted_iota(jnp.int32, (c, c), 1)
    tril = ri >= ci
    strict = ri > ci
    ones_b = jnp.ones((2 * c, dk), BF16)
    triu_b = jnp.where(ri <= ci, 1.0, 0.0).astype(BF16)
    eye = jnp.where(ri == ci, 1.0, 0.0)

    def split(x):
        hi = x.astype(BF16)
        return hi, (x - hi.astype(F32)).astype(BF16)

    def prepare(nn):
        pairs = [(g, nn * GDN_UNROLL + j) for j in range(GDN_UNROLL) for g in range(GDN_GROUP)]
        ps = range(len(pairs))
        rows = [pl.ds(pl.multiple_of(n * c, c), c) for _, n in pairs]
        q = [qs[rows[i], g * dk:(g + 1) * dk] for i, (g, _) in enumerate(pairs)]
        k = [ks[rows[i], g * dk:(g + 1) * dk] for i, (g, _) in enumerate(pairs)]
        v = [vs[rows[i], g * dk:(g + 1) * dk] for i, (g, _) in enumerate(pairs)]
        g_row = [-a_gain[g] * _softplus(a_ref[0, g, pl.ds(n, 1), :] + dtb[g]) for g, n in pairs]
        b_row = [_sigmoid(b_ref[0, g, pl.ds(n, 1), :]) for g, n in pairs]
        col_src = [jnp.concatenate([jnp.where(tril, jnp.broadcast_to(g_row[i], (c, c)), 0.0),
                                    jnp.where(ri == ci, jnp.broadcast_to(b_row[i], (c, c)), 0.0)],
                                   axis=0) for i in ps]
        col2 = [_dot(jnp.concatenate(split(col_src[i]), axis=1), ones_b) for i in ps]
        gc = [col2[i][:c] for i in ps]
        beta = [col2[i][c:] for i in ps]
        grs = [jnp.concatenate(split(jnp.broadcast_to(g_row[i], (8, c))), axis=0) for i in ps]
        gr2 = [_dot(grs[i], triu_b) for i in ps]
        gc_row = [gr2[i][0:1] + gr2[i][8:9] for i in ps]
        yield
        decay = [jnp.where(tril, jnp.exp(jnp.where(tril, gc[i][:, :c] - gc_row[i], 0.0)), 0.0)
                 for i in ps]
        kbf = [k[i].astype(BF16) for i in ps]
        k_beta = [k[i] * beta[i] for i in ps]
        u = [-jnp.where(strict, _dot_nt(k_beta[i].astype(BF16), kbf[i]) * decay[i], 0.0) for i in ps]
        mb = [u[i].astype(BF16) for i in ps]
        m2 = [_dot(mb[i], mb[i]) for i in ps]
        yield
        for level in range(5):
            mb = [m2[i].astype(BF16) for i in ps]
            if level < 4:
                both = [_dot(jnp.concatenate([mb[i], u[i].astype(BF16)], axis=0), mb[i]) for i in ps]
                u = [u[i] + m2[i] + both[i][c:] for i in ps]
                m2 = [both[i][:c] for i in ps]
            else:
                u = [u[i] + m2[i] + _dot(u[i].astype(BF16), mb[i]) for i in ps]
            yield
        tb = [(eye + u[i]).astype(BF16) for i in ps]
        egc = [jnp.exp(gc[i]) for i in ps]
        kv_b = [_dot(tb[i], jnp.concatenate([(k_beta[i] * egc[i]).astype(BF16),
                                             (v[i] * beta[i]).astype(BF16)], axis=1)).astype(BF16)
                for i in ps]
        attn_b = [jnp.where(tril, _dot_nt(q[i].astype(BF16), kbf[i]) * decay[i], 0.0).astype(BF16)
                  for i in ps]
        yield
        g_last = [gc[i][c - 1:c, :] for i in ps]
        kw_b = [(k[i] * jnp.exp(g_last[i] - gc[i])).astype(BF16) for i in ps]
        mb_out = [_dot_tn(kw_b[i], kv_b[i]) for i in ps]
        qo_out = [_dot(attn_b[i], kv_b[i]) for i in ps]
        m_out = [mb_out[i][:, :dk].astype(BF16) for i in ps]
        b_out = [mb_out[i][:, dk:] for i in ps]
        qp_out = [(q[i] * egc[i] - qo_out[i][:, :dk]).astype(BF16) for i in ps]
        op_out = [qo_out[i][:, dk:] for i in ps]
        for i, (g, n) in enumerate(pairs):
            m_scr[g, n] = m_out[i]
            b_scr[g, n] = b_out[i]
            qp_scr[g, n] = qp_out[i]
            op_scr[g, n] = op_out[i]
            eg_scr[g, n] = jnp.broadcast_to(jnp.exp(g_last[i]), (8, dk))

    def scan(n, states):
        r0 = pl.multiple_of(n * c, c)
        gs = range(GDN_GROUP)
        sb = [states[g].astype(BF16) for g in gs]
        zc = [z_ref[0, pl.ds(r0, c), g * dk:(g + 1) * dk].astype(F32) for g in gs]
        o = [_dot(qp_scr[g, n], sb[g]) + op_scr[g, n] for g in gs]
        new = tuple(states[g] * eg_scr[g, n][0:1] - _dot(m_scr[g, n], sb[g]) + b_scr[g, n]
                    for g in gs)
        for g in gs:
            o_ref[0, pl.ds(r0, c), g * dk:(g + 1) * dk] = (
                _rms(o[g], onw_ref[...]) * (zc[g] * _sigmoid(zc[g]))).astype(o_ref.dtype)
        return new

    ngroup = s // (c * GDN_UNROLL)
    for _ in prepare(0):
        pass

    def body(nn, states):
        stages = prepare(nn)
        for step in range(GDN_UNROLL):
            next(stages, None)
            states = scan((nn - 1) * GDN_UNROLL + step, states)
        for _ in stages:
            pass
        return states

    states = lax.fori_loop(1, ngroup, body,
                           tuple(jnp.zeros((dk, dk), F32) for _ in range(GDN_GROUP)))
    lax.fori_loop((ngroup - 1) * GDN_UNROLL, ngroup * GDN_UNROLL, scan, states)


def _gdn(proj3, conv_w, a_log, dt_bias, a_rows, b_rows, out_norm_w):
    b, s, _ = proj3.shape
    n = s // GDN_CHUNK
    c = GDN_CHUNK
    gg = GDN_GROUP
    w = gg * HEAD_DIM
    assert s % (c * GDN_UNROLL) == 0, s
    col = lambda base: pl.BlockSpec((1, s, w), lambda bi, h: (bi, 0, base // gg + h))
    cw = lambda base: pl.BlockSpec((GDN_CONV, w), lambda bi, h: (0, base // gg + h))
    smem = pl.BlockSpec(memory_space=pltpu.SMEM)
    return pl.pallas_call(
        _gdn_kernel,
        out_shape=jax.ShapeDtypeStruct((b, s, HEADS * HEAD_DIM), BF16),
        grid=(b, HEADS // gg),
        in_specs=[
            smem, smem,
            col(3 * HEADS), col(4 * HEADS), col(5 * HEADS), col(6 * HEADS),
            cw(0), cw(HEADS), cw(2 * HEADS),
            pl.BlockSpec((1, gg, n, c), lambda bi, h: (bi, h, 0, 0)),
            pl.BlockSpec((1, gg, n, c), lambda bi, h: (bi, h, 0, 0)),
            pl.BlockSpec((1, HEAD_DIM), lambda bi, h: (0, 0)),
        ],
        out_specs=pl.BlockSpec((1, s, w), lambda bi, h: (bi, 0, h)),
        scratch_shapes=[
            pltpu.VMEM((s + 8, w), F32), pltpu.VMEM((s, w), F32), pltpu.VMEM((s, w), F32),
            pltpu.VMEM((s, w), F32),
            pltpu.VMEM((gg, n, HEAD_DIM, HEAD_DIM), BF16), pltpu.VMEM((gg, n, HEAD_DIM, HEAD_DIM), F32),
            pltpu.VMEM((gg, n, c, HEAD_DIM), BF16), pltpu.VMEM((gg, n, c, HEAD_DIM), F32),
            pltpu.VMEM((gg, n, 8, HEAD_DIM), F32),
        ],
        compiler_params=pltpu.CompilerParams(
            dimension_semantics=("parallel", "parallel"), vmem_limit_bytes=VMEM_LIMIT),
        name="gdn",
    )(a_log, dt_bias, proj3, proj3, proj3, proj3, conv_w, conv_w, conv_w,
      a_rows, b_rows, out_norm_w)


def _merge_kernel(osb_ref, ogdn_ref, gsb_ref, ggdn_ref, x_ref, wsb_ref, wgdn_ref, wout_ref,
                  fnw_ref, wq_ref, k1_ref, k2_ref, x1_ref, h2t_ref, s1_ref, s2_ref):
    y_sb = _dot(osb_ref[...], wsb_ref[...])
    y_gdn = _dot(ogdn_ref[...], wgdn_ref[...])
    merged = (_sigmoid(gsb_ref[...].astype(F32)) * y_sb
              + _sigmoid(ggdn_ref[...].astype(F32)) * y_gdn)
    x1 = x_ref[...] + _dot(merged.astype(BF16), wout_ref[...])
    x1_ref[...] = x1
    h2f = _rms(x1, fnw_ref[...])
    h2 = h2f.astype(BF16)
    h2t_ref[...] = h2f.T.astype(BF16)
    q = _dot(h2, wq_ref[...]).astype(BF16)
    for hh in range(PEER_HEADS):
        base = hh * 2 * PEER_HALF
        s1_ref[hh] = _dot_nt(k1_ref[hh], q[:, base:base + PEER_HALF])
        s2_ref[hh] = _dot_nt(k2_ref[hh], q[:, base + PEER_HALF:base + 2 * PEER_HALF])


def _merge(o_sb, o_gdn, proj_main, x2d, w_sb, w_gdn, w_out, ffn_w, w_q, keys1, keys2, tm):
    n = x2d.shape[0]
    full = lambda shape: pl.BlockSpec(shape, lambda i: (0,) * len(shape))
    tok = lambda width, cb=0: pl.BlockSpec((tm, width), lambda i, cb=cb: (i, cb))
    st_spec = pl.BlockSpec((PEER_HEADS, PEER_N_KEYS, tm), lambda i: (0, 0, i))
    return pl.pallas_call(
        _merge_kernel,
        out_shape=(jax.ShapeDtypeStruct((n, D_MODEL), F32),
                   jax.ShapeDtypeStruct((D_MODEL, n), BF16),
                   jax.ShapeDtypeStruct((PEER_HEADS, PEER_N_KEYS, n), F32),
                   jax.ShapeDtypeStruct((PEER_HEADS, PEER_N_KEYS, n), F32)),
        grid=(n // tm,),
        in_specs=[
            tok(D_MODEL), tok(D_MODEL), tok(D_MODEL, 7), tok(D_MODEL, 8), tok(D_MODEL),
            full((D_MODEL, D_MODEL)), full((D_MODEL, D_MODEL)), full((D_MODEL, D_MODEL)),
            full((1, D_MODEL)), full((D_MODEL, 2 * PEER_HALF * PEER_HEADS)),
            full((PEER_HEADS, PEER_N_KEYS, PEER_HALF)), full((PEER_HEADS, PEER_N_KEYS, PEER_HALF)),
        ],
        out_specs=(tok(D_MODEL), pl.BlockSpec((D_MODEL, tm), lambda i: (0, i)), st_spec, st_spec),
        compiler_params=pltpu.CompilerParams(
            dimension_semantics=("parallel",), vmem_limit_bytes=VMEM_LIMIT),
        name="merge",
    )(o_sb, o_gdn, proj_main, proj_main, x2d, w_sb, w_gdn, w_out, ffn_w, w_q, keys1, keys2)


def _top_ranked(s, k):
    assert k <= 16
    big = 2.0 ** 100
    s = jnp.maximum(s, -0.5 * big)
    vals = []
    for r in range(k):
        m = jnp.max(s, axis=0, keepdims=True)
        vals.append(m)
        s = jnp.where(s == m, -big * (1.0 + r / 16.0), s)
    rank = jnp.where(s <= -big, (s * (-1.0 / big) - 1.0) * 16.0, float(k))
    return jnp.concatenate(vals, axis=0), rank


def _topk_kernel(s1_ref, s2_ref, n1_ref, p1_ref, r2_ref, p2_ref, *, tl):
    kk = PEER_TOPK

    def sub(bi, _):
        l0 = pl.multiple_of(bi * LANES, LANES)
        s1 = s1_ref[0, :, pl.ds(l0, LANES)]
        s2 = s2_ref[0, :, pl.ds(l0, LANES)]
        v1, rank1 = _top_ranked(s1, kk)
        v2, rank2 = _top_ranked(s2, kk)
        jrow8 = lax.broadcasted_iota(jnp.int32, (8, LANES), 0)
        blocks = [v1[0:1] + v2]
        for i in range(1, 8):
            blocks.append(v1[i:i + 1] + jnp.where(jrow8 < kk // (i + 1), v2[0:8], NEG_INF))
        blocks.append(v1[8:16] + v2[0:1])
        cand = jnp.concatenate(blocks, axis=0)
        top = cand[0:1]
        work = cand
        tau = top
        for _ in range(kk):
            tau = jnp.max(work, axis=0, keepdims=True)
            work = jnp.where(work == tau, NEG_INF, work)
        sel = cand >= tau
        zsum = jnp.sum(jnp.where(sel, jnp.exp(jnp.where(sel, cand - top, 0.0)), 0.0),
                       axis=0, keepdims=True)
        counts = [jnp.sum(jnp.where(blocks[i] >= tau, 1.0, 0.0), axis=0, keepdims=True)
                  for i in range(8)]
        counts.append(jnp.where(blocks[8] >= tau, 1.0, 0.0))
        nrank = jnp.concatenate(counts, axis=0)
        n1 = jnp.zeros(s1.shape, F32)
        for i in range(kk):
            n1 = jnp.where(rank1 == float(i), nrank[i:i + 1], n1)
        n1_ref[0, :, pl.ds(l0, LANES)] = n1
        p1_ref[0, :, pl.ds(l0, LANES)] = jnp.exp(s1 - v1[0:1]) / zsum
        r2_ref[:, pl.ds(l0, LANES)] = rank2.astype(BF16)
        p2_ref[:, pl.ds(l0, LANES)] = jnp.exp(s2 - v2[0:1]).astype(BF16)
        return 0

    lax.fori_loop(0, tl // LANES, sub, 0)


def _peer_topk(s1t, s2t, tl):
    hh, kk, n = s1t.shape
    spec = pl.BlockSpec((1, kk, tl), lambda i, h: (h, 0, i))
    f32 = jax.ShapeDtypeStruct((hh, kk, n), F32)
    b16 = jax.ShapeDtypeStruct((hh * kk, n), BF16)
    spec2 = pl.BlockSpec((kk, tl), lambda i, h: (h, i))
    return pl.pallas_call(
        functools.partial(_topk_kernel, tl=tl),
        out_shape=(f32, f32, b16, b16),
        grid=(n // tl, hh),
        in_specs=[spec, spec],
        out_specs=(spec, spec, spec2, spec2),
        compiler_params=pltpu.CompilerParams(
            dimension_semantics=("parallel", "parallel"), vmem_limit_bytes=VMEM_LIMIT),
        name="peer_topk",
    )(s1t, s2t)


def _dense_kernel(h2t_ref, u_ref, vt_ref, r2_ref, p2_ref, n1_ref, p1_ref, x1_ref, o_ref,
                  acc_scr, w_scr, r2_scr, p2_scr, *, ec):
    j = pl.program_id(1)
    nk = PEER_N_KEYS
    tl = h2t_ref.shape[1]
    pk = 16

    @pl.when(j == 0)
    def _():
        acc_scr[...] = jnp.zeros_like(acc_scr)
        r2_scr[...] = r2_ref[...]
        p2_scr[...] = p2_ref[...]

    pre = _dot(u_ref[...], h2t_ref[...])
    for e in range(ec // nk):
        e1 = j * (ec // nk) + e
        n1_rows = [jnp.broadcast_to(n1_ref[hh, pl.ds(e1, 1), :], (pk, tl)).astype(BF16)
                   for hh in range(PEER_HEADS)]
        p1_rows = [jnp.broadcast_to(p1_ref[hh, pl.ds(e1, 1), :], (pk, tl)).astype(BF16)
                   for hh in range(PEER_HEADS)]
        for tv in range(tl // LANES):
            lanes = slice(tv * LANES, (tv + 1) * LANES)
            g = None
            for hh in range(PEER_HEADS):
                keys = slice(hh * nk, (hh + 1) * nk)
                n1 = jnp.concatenate([n1_rows[hh][:, lanes]] * (nk // pk), axis=0)
                p1 = jnp.concatenate([p1_rows[hh][:, lanes]] * (nk // pk), axis=0)
                p2 = p2_scr[keys, lanes]
                term = jnp.where(r2_scr[keys, lanes] < n1, p2, jnp.zeros_like(p2)) * p1
                g = term if g is None else g + term
            blk = pre[e * nk:(e + 1) * nk, lanes].astype(BF16)
            act = (0.5 * blk) * (1.0 + lax.erf(blk * (2.0 ** -0.5)))
            w_scr[pl.ds(e * nk, nk), lanes] = act * g
    acc_scr[...] += _dot(vt_ref[...], w_scr[...])

    @pl.when(j == pl.num_programs(1) - 1)
    def _():
        o_ref[...] = x1_ref[...] + acc_scr[...].T


def _peer_dense(h2t, u_tab, vt_tab, r2, p2, n1, p1, x1, tl, ec):
    d, n = h2t.shape
    ne = u_tab.shape[0]
    hspec = pl.BlockSpec((PEER_HEADS, PEER_N_KEYS, tl), lambda i, j: (0, 0, i))
    bspec = pl.BlockSpec((PEER_HEADS * PEER_N_KEYS, tl), lambda i, j: (0, i))
    return pl.pallas_call(
        functools.partial(_dense_kernel, ec=ec),
        out_shape=jax.ShapeDtypeStruct((n, d), F32),
        grid=(n // tl, ne // ec),
        in_specs=[
            pl.BlockSpec((d, tl), lambda i, j: (0, i)),
            pl.BlockSpec((ec, d), lambda i, j: (j, 0)),
            pl.BlockSpec((d, ec), lambda i, j: (0, j)),
            bspec, bspec, hspec, hspec,
            pl.BlockSpec((tl, d), lambda i, j: (i, 0)),
        ],
        out_specs=pl.BlockSpec((tl, d), lambda i, j: (i, 0)),
        scratch_shapes=[pltpu.VMEM((d, tl), F32), pltpu.VMEM((ec, tl), BF16),
                        pltpu.VMEM((PEER_HEADS * PEER_N_KEYS, tl), BF16),
                        pltpu.VMEM((PEER_HEADS * PEER_N_KEYS, tl), BF16)],
        compiler_params=pltpu.CompilerParams(
            dimension_semantics=("parallel", "arbitrary"), vmem_limit_bytes=VMEM_LIMIT),
        name="peer_dense",
    )(h2t, u_tab, vt_tab, r2, p2, n1, p1, x1)


def _tile(n, pref):
    t = min(n, pref)
    assert n % t == 0, (n, t)
    return t


def kernel(x, mix_norm_w, w_in, sb_q_norm_w, sb_k_norm_w, gdn_conv_w, gdn_a_log, gdn_dt_bias,
           gdn_out_norm_w, w_branch_sb, w_branch_gdn, w_out, ffn_norm_w, peer_w_q, peer_keys1,
           peer_keys2, peer_u, peer_v):
    b, s, d = x.shape
    n = b * s
    depth = w_in.shape[0]
    sb_w = HEADS * HEAD_DIM
    small0 = 7 * sb_w
    small1 = small0 + 2 * HEADS
    nchunk = s // GDN_CHUNK
    x2d = x.reshape(n, d)
    for l in range(depth):
        w_main = jnp.concatenate([w_in[l][:, :small0], w_in[l][:, small1:]], axis=1).astype(BF16)
        w_small = jnp.pad(w_in[l][:, small0:small1], ((0, 0), (0, LANES - 2 * HEADS))).astype(BF16)
        proj_main, proj_small = _in_proj(x2d, mix_norm_w[l][None], w_main, w_small,
                                         _tile(n, IN_PROJ_ROWS), IN_PROJ_COLS)
        proj3 = proj_main.reshape(b, s, MAIN_WIDTH)

        o_sb = _sb_attention(proj3, sb_q_norm_w[l][None], sb_k_norm_w[l][None],
                             _tile(s, SB_QUERY_ROWS))

        ba = proj_small[:, :2 * HEADS].reshape(b, nchunk, GDN_CHUNK, 2 * HEADS)
        ba = ba.transpose(0, 3, 1, 2)
        o_gdn = _gdn(proj3, gdn_conv_w[l], gdn_a_log[l], gdn_dt_bias[l], ba[:, HEADS:], ba[:, :HEADS],
                     gdn_out_norm_w[l][None])

        x1, h2t, s1t, s2t = _merge(
            o_sb.reshape(n, sb_w), o_gdn.reshape(n, sb_w), proj_main, x2d,
            w_branch_sb[l].astype(BF16), w_branch_gdn[l].astype(BF16), w_out[l].astype(BF16),
            ffn_norm_w[l][None], peer_w_q[l].astype(BF16),
            peer_keys1[l].astype(BF16), peer_keys2[l].astype(BF16), _tile(n, MERGE_ROWS))

        n1, p1, r2, p2 = _peer_topk(s1t, s2t, _tile(n, TOPK_TOKENS))
        x2d = _peer_dense(h2t, peer_u[l].astype(BF16), peer_v[l].astype(BF16).T,
                          r2, p2, n1, p1, x1, _tile(n, DENSE_TOKENS), DENSE_EXPERTS)
    return x2d.reshape(b, s, d)
```

```python
import functools

import jax
import jax.numpy as jnp
import numpy as np
from jax import lax
from jax.experimental import pallas as pl
from jax.experimental.pallas import tpu as pltpu

F32 = jnp.float32
BF16 = jnp.bfloat16

D_MODEL = 1024
HEADS = 8
HEAD_DIM = 128
LANES = 128
GDN_CONV = 4
GDN_CHUNK = 64
PEER_HEADS = 8
PEER_N_KEYS = 128
PEER_HALF = 128
PEER_TOPK = 16
EPS = 1e-6
MAIN_WIDTH = 9 * D_MODEL
VMEM_LIMIT = 56 * 1024 * 1024
SB_LOG_UNDERFLOW = -104.0
SB_GROUP = 8
GDN_GROUP = 2
GDN_UNROLL = 8
TOPK_UNROLL = 8
IN_PROJ_ROWS, IN_PROJ_COLS = 1024, 4608
SB_QUERY_ROWS = 256
MERGE_ROWS = 512
TOPK_TOKENS = 1024
DENSE_TOKENS, DENSE_EXPERTS = 512, 2048
NEG_INF = float("-inf")
POS_INF = float("inf")


def _sigmoid(x):
    return 1.0 / (1.0 + jnp.exp(-x))


def _softplus(x):
    return jnp.maximum(x, 0.0) + jnp.log1p(jnp.exp(-jnp.abs(x)))


def _rms(x, w):
    return x * lax.rsqrt(jnp.mean(x * x, axis=-1, keepdims=True) + EPS) * w


def _dot(a, b):
    return jnp.dot(a, b, preferred_element_type=F32)


def _dot_nt(a, b):
    return lax.dot_general(a, b, (((1,), (1,)), ((), ())), preferred_element_type=F32)


def _dot_tn(a, b):
    return lax.dot_general(a, b, (((0,), (0,)), ((), ())), preferred_element_type=F32)


def _inproj_kernel(x_ref, nw_ref, w_ref, ws_ref, o_ref, os_ref, h_scr):
    j = pl.program_id(1)

    @pl.when(j == 0)
    def _():
        h = _rms(x_ref[...], nw_ref[...]).astype(BF16)
        h_scr[...] = h
        os_ref[...] = _dot(h, ws_ref[...])

    o_ref[...] = _dot(h_scr[...], w_ref[...]).astype(o_ref.dtype)


def _in_proj(x2d, norm_w, w_main, w_small, tm, tn):
    n = x2d.shape[0]
    return pl.pallas_call(
        _inproj_kernel,
        out_shape=(jax.ShapeDtypeStruct((n, MAIN_WIDTH), BF16),
                   jax.ShapeDtypeStruct((n, LANES), F32)),
        grid=(n // tm, MAIN_WIDTH // tn),
        in_specs=[
            pl.BlockSpec((tm, D_MODEL), lambda i, j: (i, 0)),
            pl.BlockSpec((1, D_MODEL), lambda i, j: (0, 0)),
            pl.BlockSpec((D_MODEL, tn), lambda i, j: (0, j)),
            pl.BlockSpec((D_MODEL, LANES), lambda i, j: (0, 0)),
        ],
        out_specs=(pl.BlockSpec((tm, tn), lambda i, j: (i, j)),
                   pl.BlockSpec((tm, LANES), lambda i, j: (i, 0))),
        scratch_shapes=[pltpu.VMEM((tm, D_MODEL), BF16)],
        compiler_params=pltpu.CompilerParams(
            dimension_semantics=("parallel", "arbitrary"), vmem_limit_bytes=VMEM_LIMIT),
        name="in_proj",
    )(x2d, norm_w, w_main, w_small)


def _sb_kernel(q_ref, k_ref, v_ref, qw_ref, kw_ref, tri_ref, o_ref, kn_scr, *, tq):
    i = pl.program_id(2)
    kb = LANES
    dh = HEAD_DIM
    heads = range(SB_GROUP)

    @pl.when(i == 0)
    def _():
        for g in heads:
            kn_scr[:, g * dh:(g + 1) * dh] = _rms(
                k_ref[0, :, g * dh:(g + 1) * dh].astype(F32), kw_ref[...]).astype(BF16)

    scale = dh ** -0.5
    qn = [(_rms(q_ref[0, :, g * dh:(g + 1) * dh].astype(F32), qw_ref[...]) * scale).astype(BF16)
          for g in heads]
    row = i * tq + lax.broadcasted_iota(jnp.int32, (tq, kb), 0)
    col0 = lax.broadcasted_iota(jnp.int32, (tq, kb), 1)
    npair = (i + 1) * (tq // (2 * kb))

    def cond(carry):
        p, cs, _ = carry
        cmax = cs[0]
        for g in heads[1:]:
            cmax = jnp.maximum(cmax, cs[g])
        return jnp.logical_and(p < npair, jnp.max(cmax) > SB_LOG_UNDERFLOW)

    def body(carry):
        p, cs, accs = carry
        k0 = pl.multiple_of((npair - 1 - p) * (2 * kb), 2 * kb)
        masks = [(col0 + (k0 + half * kb)) < row for half in range(2)]
        z2 = [_dot_nt(qn[g], kn_scr[pl.ds(k0, 2 * kb), g * dh:(g + 1) * dh]) for g in heads]
        cats, logits = [], []
        for g in heads:
            cat_g, logit_g = [], []
            for half in range(2):
                z = z2[g][:, half * kb:(half + 1) * kb]
                sp = jnp.maximum(z, 0.0) + jnp.log(1.0 + jnp.exp(-jnp.abs(z)))
                lneg = jnp.where(masks[half], -sp, 0.0)
                hi = lneg.astype(BF16)
                lo = (lneg - hi.astype(F32)).astype(BF16)
                cat_g.append(jnp.concatenate([hi, lo], axis=1))
                logit_g.append(z - sp)
            cats.append(jnp.concatenate(cat_g, axis=0))
            logits.append(logit_g)
        res = [_dot(cats[g], tri_ref[...]) for g in heads]
        a2, c_new = [], []
        for g in heads:
            c_early = cs[g] + res[g][tq:, kb:]
            a_late = jnp.where(masks[1], jnp.exp(logits[g][1] + res[g][tq:, :kb] + cs[g]), 0.0)
            a_early = jnp.where(masks[0], jnp.exp(logits[g][0] + res[g][:tq, :kb] + c_early), 0.0)
            a2.append(jnp.concatenate([a_early, a_late], axis=1).astype(BF16))
            c_new.append(c_early + res[g][:tq, kb:])
        accs = tuple(accs[g] + _dot(a2[g], v_ref[0, pl.ds(k0, 2 * kb), g * dh:(g + 1) * dh])
                     for g in heads)
        return p + 1, tuple(c_new), accs

    init = (jnp.int32(0), tuple(jnp.zeros((tq, kb), F32) for _ in heads),
            tuple(jnp.zeros((tq, dh), F32) for _ in heads))
    _, _, accs = lax.while_loop(cond, body, init)
    for g in heads:
        o_ref[0, :, g * dh:(g + 1) * dh] = accs[g].astype(o_ref.dtype)


def _sb_tri():
    r = np.arange(2 * LANES)[:, None] % LANES
    c = np.arange(2 * LANES)[None, :]
    m = np.where(c < LANES, r > c, True)
    return jnp.asarray(m, dtype=BF16)


def _sb_attention(proj3, qw, kw, tq):
    b, s, _ = proj3.shape
    w = SB_GROUP * HEAD_DIM
    ng = HEADS // SB_GROUP
    return pl.pallas_call(
        functools.partial(_sb_kernel, tq=tq),
        out_shape=jax.ShapeDtypeStruct((b, s, HEADS * HEAD_DIM), BF16),
        grid=(b, ng, s // tq),
        in_specs=[
            pl.BlockSpec((1, tq, w), lambda bi, h, i: (bi, i, h)),
            pl.BlockSpec((1, s, w), lambda bi, h, i: (bi, 0, ng + h)),
            pl.BlockSpec((1, s, w), lambda bi, h, i: (bi, 0, 2 * ng + h)),
            pl.BlockSpec((1, HEAD_DIM), lambda bi, h, i: (0, 0)),
            pl.BlockSpec((1, HEAD_DIM), lambda bi, h, i: (0, 0)),
            pl.BlockSpec((2 * LANES, 2 * LANES), lambda bi, h, i: (0, 0)),
        ],
        out_specs=pl.BlockSpec((1, tq, w), lambda bi, h, i: (bi, i, h)),
        scratch_shapes=[pltpu.VMEM((s, w), BF16)],
        compiler_params=pltpu.CompilerParams(
            dimension_semantics=("parallel", "parallel", "arbitrary"),
            vmem_limit_bytes=VMEM_LIMIT),
        name="sb_attn",
    )(proj3, proj3, proj3, qw, kw, _sb_tri())


def _gdn_kernel(alog_ref, dtb_ref, gq_ref, gk_ref, gv_ref, z_ref, cwq_ref, cwk_ref, cwv_ref,
                a_ref, b_ref, onw_ref, o_ref,
                xpad, qs, ks, vs, m_scr, b_scr, qp_scr, op_scr, eg_scr):
    hg = pl.program_id(1)
    s = qs.shape[0]
    c = GDN_CHUNK
    dk = HEAD_DIM
    w = GDN_GROUP * dk
    pad = 8

    def conv_silu(src_ref, cw_ref):
        xpad[pl.ds(0, pad), :] = jnp.zeros((pad, w), F32)
        xpad[pl.ds(pad, s), :] = src_ref[0].astype(F32)
        y = cw_ref[0:1, :] * xpad[pl.ds(pad - 3, s), :]
        for t in range(1, GDN_CONV):
            y = y + cw_ref[t:t + 1, :] * xpad[pl.ds(pad - 3 + t, s), :]
        return y * _sigmoid(y)

    def l2n(x):
        return x * lax.rsqrt(jnp.sum(x * x, axis=-1, keepdims=True) + EPS)

    yq = conv_silu(gq_ref, cwq_ref)
    for g in range(GDN_GROUP):
        qs[:, g * dk:(g + 1) * dk] = l2n(yq[:, g * dk:(g + 1) * dk]) * (dk ** -0.5)
    yk = conv_silu(gk_ref, cwk_ref)
    for g in range(GDN_GROUP):
        ks[:, g * dk:(g + 1) * dk] = l2n(yk[:, g * dk:(g + 1) * dk])
    vs[...] = conv_silu(gv_ref, cwv_ref)

    a_gain = [jnp.exp(jnp.full((1, 1), alog_ref[hg * GDN_GROUP + g], F32)) for g in range(GDN_GROUP)]
    dtb = [jnp.full((1, 1), dtb_ref[hg * GDN_GROUP + g], F32) for g in range(GDN_GROUP)]

    ri = lax.broadcasted_iota(jnp.int32, (c, c), 0)
    ci = lax.broadcasted_iota(jnp.int32, (c, c), 1)
    tril = ri >= ci
    strict = ri > ci
    ones_b = jnp.ones((2 * c, dk), BF16)
    triu_b = jnp.where(ri <= ci, 1.0, 0.0).astype(BF16)
    eye = jnp.where(ri == ci, 1.0, 0.0)

    def split(x):
        hi = x.astype(BF16)
        return hi, (x - hi.astype(F32)).astype(BF16)

    def prepare(nn):
        pairs = [(g, nn * GDN_UNROLL + j) for j in range(GDN_UNROLL) for g in range(GDN_GROUP)]
        ps = range(len(pairs))
        rows = [pl.ds(pl.multiple_of(n * c, c), c) for _, n in pairs]
        q = [qs[rows[i], g * dk:(g + 1) * dk] for i, (g, _) in enumerate(pairs)]
        k = [ks[rows[i], g * dk:(g + 1) * dk] for i, (g, _) in enumerate(pairs)]
        v = [vs[rows[i], g * dk:(g + 1) * dk] for i, (g, _) in enumerate(pairs)]
        g_row = [-a_gain[g] * _softplus(a_ref[0, g, pl.ds(n, 1), :] + dtb[g]) for g, n in pairs]
        b_row = [_sigmoid(b_ref[0, g, pl.ds(n, 1), :]) for g, n in pairs]
        col_src = [jnp.concatenate([jnp.where(tril, jnp.broadcast_to(g_row[i], (c, c)), 0.0),
                                    jnp.where(ri == ci, jnp.broadcast_to(b_row[i], (c, c)), 0.0)],
                                   axis=0) for i in ps]
        col2 = [_dot(jnp.concatenate(split(col_src[i]), axis=1), ones_b) for i in ps]
        gc = [col2[i][:c] for i in ps]
        beta = [col2[i][c:] for i in ps]
        grs = [jnp.concatenate(split(jnp.broadcast_to(g_row[i], (8, c))), axis=0) for i in ps]
        gr2 = [_dot(grs[i], triu_b) for i in ps]
        gc_row = [gr2[i][0:1] + gr2[i][8:9] for i in ps]
        yield
        decay = [jnp.where(tril, jnp.exp(jnp.where(tril, gc[i][:, :c] - gc_row[i], 0.0)), 0.0)
                 for i in ps]
        kbf = [k[i].astype(BF16) for i in ps]
        k_beta = [k[i] * beta[i] for i in ps]
        u = [-jnp.where(strict, _dot_nt(k_beta[i].astype(BF16), kbf[i]) * decay[i], 0.0) for i in ps]
        mb = [u[i].astype(BF16) for i in ps]
        m2 = [_dot(mb[i], mb[i]) for i in ps]
        yield
        for level in range(5):
            mb = [m2[i].astype(BF16) for i in ps]
            if level < 4:
                both = [_dot(jnp.concatenate([mb[i], u[i].astype(BF16)], axis=0), mb[i]) for i in ps]
                u = [u[i] + m2[i] + both[i][c:] for i in ps]
                m2 = [both[i][:c] for i in ps]
            else:
                u = [u[i] + m2[i] + _dot(u[i].astype(BF16), mb[i]) for i in ps]
            yield
        tb = [(eye + u[i]).astype(BF16) for i in ps]
        egc = [jnp.exp(gc[i]) for i in ps]
        kv_b = [_dot(tb[i], jnp.concatenate([(k_beta[i] * egc[i]).astype(BF16),
                                             (v[i] * beta[i]).astype(BF16)], axis=1)).astype(BF16)
                for i in ps]
        attn_b = [jnp.where(tril, _dot_nt(q[i].astype(BF16), kbf[i]) * decay[i], 0.0).astype(BF16)
                  for i in ps]
        yield
        g_last = [gc[i][c - 1:c, :] for i in ps]
        kw_b = [(k[i] * jnp.exp(g_last[i] - gc[i])).astype(BF16) for i in ps]
        mb_out = [_dot_tn(kw_b[i], kv_b[i]) for i in ps]
        qo_out = [_dot(attn_b[i], kv_b[i]) for i in ps]
        m_out = [mb_out[i][:, :dk].astype(BF16) for i in ps]
        b_out = [mb_out[i][:, dk:] for i in ps]
        qp_out = [(q[i] * egc[i] - qo_out[i][:, :dk]).astype(BF16) for i in ps]
        op_out = [qo_out[i][:, dk:] for i in ps]
        for i, (g, n) in enumerate(pairs):
            m_scr[g, n] = m_out[i]
            b_scr[g, n] = b_out[i]
            qp_scr[g, n] = qp_out[i]
            op_scr[g, n] = op_out[i]
            eg_scr[g, n] = jnp.broadcast_to(jnp.exp(g_last[i]), (8, dk))

    def scan(n, states):
        r0 = pl.multiple_of(n * c, c)
        gs = range(GDN_GROUP)
        sb = [states[g].astype(BF16) for g in gs]
        zc = [z_ref[0, pl.ds(r0, c), g * dk:(g + 1) * dk].astype(F32) for g in gs]
        o = [_dot(qp_scr[g, n], sb[g]) + op_scr[g, n] for g in gs]
        new = tuple(states[g] * eg_scr[g, n][0:1] - _dot(m_scr[g, n], sb[g]) + b_scr[g, n]
                    for g in gs)
        for g in gs:
            o_ref[0, pl.ds(r0, c), g * dk:(g + 1) * dk] = (
                _rms(o[g], onw_ref[...]) * (zc[g] * _sigmoid(zc[g]))).astype(o_ref.dtype)
        return new

    ngroup = s // (c * GDN_UNROLL)
    for _ in prepare(0):
        pass

    def body(nn, states):
        stages = prepare(nn)
        for step in range(GDN_UNROLL):
            next(stages, None)
            states = scan((nn - 1) * GDN_UNROLL + step, states)
        for _ in stages:
            pass
        return states

    states = lax.fori_loop(1, ngroup, body,
                           tuple(jnp.zeros((dk, dk), F32) for _ in range(GDN_GROUP)))
    lax.fori_loop((ngroup - 1) * GDN_UNROLL, ngroup * GDN_UNROLL, scan, states)


def _gdn(proj3, conv_w, a_log, dt_bias, a_rows, b_rows, out_norm_w):
    b, s, _ = proj3.shape
    n = s // GDN_CHUNK
    c = GDN_CHUNK
    gg = GDN_GROUP
    w = gg * HEAD_DIM
    assert s % (c * GDN_UNROLL) == 0, s
    col = lambda base: pl.BlockSpec((1, s, w), lambda bi, h: (bi, 0, base // gg + h))
    cw = lambda base: pl.BlockSpec((GDN_CONV, w), lambda bi, h: (0, base // gg + h))
    smem = pl.BlockSpec(memory_space=pltpu.SMEM)
    return pl.pallas_call(
        _gdn_kernel,
        out_shape=jax.ShapeDtypeStruct((b, s, HEADS * HEAD_DIM), BF16),
        grid=(b, HEADS // gg),
        in_specs=[
            smem, smem,
            col(3 * HEADS), col(4 * HEADS), col(5 * HEADS), col(6 * HEADS),
            cw(0), cw(HEADS), cw(2 * HEADS),
            pl.BlockSpec((1, gg, n, c), lambda bi, h: (bi, h, 0, 0)),
            pl.BlockSpec((1, gg, n, c), lambda bi, h: (bi, h, 0, 0)),
            pl.BlockSpec((1, HEAD_DIM), lambda bi, h: (0, 0)),
        ],
        out_specs=pl.BlockSpec((1, s, w), lambda bi, h: (bi, 0, h)),
        scratch_shapes=[
            pltpu.VMEM((s + 8, w), F32), pltpu.VMEM((s, w), F32), pltpu.VMEM((s, w), F32),
            pltpu.VMEM((s, w), F32),
            pltpu.VMEM((gg, n, HEAD_DIM, HEAD_DIM), BF16), pltpu.VMEM((gg, n, HEAD_DIM, HEAD_DIM), F32),
            pltpu.VMEM((gg, n, c, HEAD_DIM), BF16), pltpu.VMEM((gg, n, c, HEAD_DIM), F32),
            pltpu.VMEM((gg, n, 8, HEAD_DIM), F32),
        ],
        compiler_params=pltpu.CompilerParams(
            dimension_semantics=("parallel", "parallel"), vmem_limit_bytes=VMEM_LIMIT),
        name="gdn",
    )(a_log, dt_bias, proj3, proj3, proj3, proj3, conv_w, conv_w, conv_w,
      a_rows, b_rows, out_norm_w)


def _merge_kernel(osb_ref, ogdn_ref, gsb_ref, ggdn_ref, x_ref, wsb_ref, wgdn_ref, wout_ref,
                  fnw_ref, wq_ref, k1_ref, k2_ref, x1_ref, h2t_ref, s1_ref, s2_ref):
    y_sb = _dot(osb_ref[...], wsb_ref[...])
    y_gdn = _dot(ogdn_ref[...], wgdn_ref[...])
    merged = (_sigmoid(gsb_ref[...].astype(F32)) * y_sb
              + _sigmoid(ggdn_ref[...].astype(F32)) * y_gdn)
    x1 = x_ref[...] + _dot(merged.astype(BF16), wout_ref[...])
    x1_ref[...] = x1
    h2f = _rms(x1, fnw_ref[...])
    h2 = h2f.astype(BF16)
    h2t_ref[...] = h2f.T.astype(BF16)
    q = _dot(h2, wq_ref[...]).astype(BF16)
    for hh in range(PEER_HEADS):
        base = hh * 2 * PEER_HALF
        s1_ref[hh] = _dot_nt(k1_ref[hh], q[:, base:base + PEER_HALF])
        s2_ref[hh] = _dot_nt(k2_ref[hh], q[:, base + PEER_HALF:base + 2 * PEER_HALF])


def _merge(o_sb, o_gdn, proj_main, x2d, w_sb, w_gdn, w_out, ffn_w, w_q, keys1, keys2, tm):
    n = x2d.shape[0]
    full = lambda shape: pl.BlockSpec(shape, lambda i: (0,) * len(shape))
    tok = lambda width, cb=0: pl.BlockSpec((tm, width), lambda i, cb=cb: (i, cb))
    st_spec = pl.BlockSpec((PEER_HEADS, PEER_N_KEYS, tm), lambda i: (0, 0, i))
    return pl.pallas_call(
        _merge_kernel,
        out_shape=(jax.ShapeDtypeStruct((n, D_MODEL), F32),
                   jax.ShapeDtypeStruct((D_MODEL, n), BF16),
                   jax.ShapeDtypeStruct((PEER_HEADS, PEER_N_KEYS, n), F32),
                   jax.ShapeDtypeStruct((PEER_HEADS, PEER_N_KEYS, n), F32)),
        grid=(n // tm,),
        in_specs=[
            tok(D_MODEL), tok(D_MODEL), tok(D_MODEL, 7), tok(D_MODEL, 8), tok(D_MODEL),
            full((D_MODEL, D_MODEL)), full((D_MODEL, D_MODEL)), full((D_MODEL, D_MODEL)),
            full((1, D_MODEL)), full((D_MODEL, 2 * PEER_HALF * PEER_HEADS)),
            full((PEER_HEADS, PEER_N_KEYS, PEER_HALF)), full((PEER_HEADS, PEER_N_KEYS, PEER_HALF)),
        ],
        out_specs=(tok(D_MODEL), pl.BlockSpec((D_MODEL, tm), lambda i: (0, i)), st_spec, st_spec),
        compiler_params=pltpu.CompilerParams(
            dimension_semantics=("parallel",), vmem_limit_bytes=VMEM_LIMIT),
        name="merge",
    )(o_sb, o_gdn, proj_main, proj_main, x2d, w_sb, w_gdn, w_out, ffn_w, w_q, keys1, keys2)


def _top_ranked(arrays, k, companion=None):
    assert k <= 16
    big = 2.0 ** 100
    work = [jnp.maximum(s, -0.5 * big) for s in arrays]
    vals = [[] for _ in arrays]
    for r in range(k):
        ms = [jnp.max(s, axis=0, keepdims=True) for s in work]
        for a, m in enumerate(ms):
            vals[a].append(m)
        work = [jnp.where(s == m, -big * (1.0 + r / 16.0), s) for s, m in zip(work, ms)]
        if companion is not None:
            next(companion, None)
    ranks = [jnp.where(s <= -big, (s * (-1.0 / big) - 1.0) * 16.0, float(k)) for s in work]
    return [(jnp.concatenate(v, axis=0), rk) for v, rk in zip(vals, ranks)]


def _topk_kernel(s1_ref, s2_ref, n1_ref, p1_ref, r2_ref, p2_ref, *, tl):
    kk = PEER_TOPK

    def select(l0, s1, s2, v1, rank1, v2, rank2):
        jrow8 = lax.broadcasted_iota(jnp.int32, (8, LANES), 0)
        blocks = [v1[0:1] + v2]
        for i in range(1, 8):
            blocks.append(v1[i:i + 1] + jnp.where(jrow8 < kk // (i + 1), v2[0:8], NEG_INF))
        blocks.append(v1[8:16] + v2[0:1])
        cand = jnp.concatenate(blocks, axis=0)
        top = cand[0:1]
        work = cand
        tau = top
        for _ in range(kk):
            tau = jnp.max(work, axis=0, keepdims=True)
            work = jnp.where(work == tau, NEG_INF, work)
            yield
        sel = cand >= tau
        zsum = jnp.sum(jnp.where(sel, jnp.exp(jnp.where(sel, cand - top, 0.0)), 0.0),
                       axis=0, keepdims=True)
        counts = [jnp.sum(jnp.where(blocks[i] >= tau, 1.0, 0.0), axis=0, keepdims=True)
                  for i in range(8)]
        counts.append(jnp.where(blocks[8] >= tau, 1.0, 0.0))
        nrank = jnp.concatenate(counts, axis=0)
        n1 = jnp.zeros(s1.shape, F32)
        for i in range(kk):
            n1 = jnp.where(rank1 == float(i), nrank[i:i + 1], n1)
        n1_ref[0, :, pl.ds(l0, LANES)] = n1
        p1_ref[0, :, pl.ds(l0, LANES)] = jnp.exp(s1 - v1[0:1]) / zsum
        r2_ref[:, pl.ds(l0, LANES)] = rank2.astype(BF16)
        p2_ref[:, pl.ds(l0, LANES)] = jnp.exp(s2 - v2[0:1]).astype(BF16)

    def sub(bi, _):
        pending = None
        for part in range(TOPK_UNROLL):
            l0 = pl.multiple_of((TOPK_UNROLL * bi + part) * LANES, LANES)
            s1 = s1_ref[0, :, pl.ds(l0, LANES)]
            s2 = s2_ref[0, :, pl.ds(l0, LANES)]
            (v1, rank1), (v2, rank2) = _top_ranked([s1, s2], kk, pending)
            if pending is not None:
                for _ in pending:
                    pass
            pending = select(l0, s1, s2, v1, rank1, v2, rank2)
        for _ in pending:
            pass
        return 0

    lax.fori_loop(0, tl // (TOPK_UNROLL * LANES), sub, 0)


def _peer_topk(s1t, s2t, tl):
    hh, kk, n = s1t.shape
    spec = pl.BlockSpec((1, kk, tl), lambda i, h: (h, 0, i))
    f32 = jax.ShapeDtypeStruct((hh, kk, n), F32)
    b16 = jax.ShapeDtypeStruct((hh * kk, n), BF16)
    spec2 = pl.BlockSpec((kk, tl), lambda i, h: (h, i))
    return pl.pallas_call(
        functools.partial(_topk_kernel, tl=tl),
        out_shape=(f32, f32, b16, b16),
        grid=(n // tl, hh),
        in_specs=[spec, spec],
        out_specs=(spec, spec, spec2, spec2),
        compiler_params=pltpu.CompilerParams(
            dimension_semantics=("parallel", "parallel"), vmem_limit_bytes=VMEM_LIMIT),
        name="peer_topk",
    )(s1t, s2t)


def _dense_kernel(h2t_ref, u_ref, vt_ref, r2_ref, p2_ref, n1_ref, p1_ref, x1_ref, o_ref,
                  acc_scr, w_scr, r2_scr, p2_scr, *, ec):
    j = pl.program_id(1)
    nk = PEER_N_KEYS
    tl = h2t_ref.shape[1]
    pk = 16

    @pl.when(j == 0)
    def _():
        acc_scr[...] = jnp.zeros_like(acc_scr)
        r2_scr[...] = r2_ref[...]
        p2_scr[...] = p2_ref[...]

    pre = _dot(u_ref[...], h2t_ref[...])
    for e in range(ec // nk):
        e1 = j * (ec // nk) + e
        n1_rows = [jnp.broadcast_to(n1_ref[hh, pl.ds(e1, 1), :], (pk, tl)).astype(BF16)
                   for hh in range(PEER_HEADS)]
        p1_rows = [jnp.broadcast_to(p1_ref[hh, pl.ds(e1, 1), :], (pk, tl)).astype(BF16)
                   for hh in range(PEER_HEADS)]
        for tv in range(tl // LANES):
            lanes = slice(tv * LANES, (tv + 1) * LANES)
            g = None
            for hh in range(PEER_HEADS):
                keys = slice(hh * nk, (hh + 1) * nk)
                n1 = jnp.concatenate([n1_rows[hh][:, lanes]] * (nk // pk), axis=0)
                p1 = jnp.concatenate([p1_rows[hh][:, lanes]] * (nk // pk), axis=0)
                p2 = p2_scr[keys, lanes]
                term = jnp.where(r2_scr[keys, lanes] < n1, p2, jnp.zeros_like(p2)) * p1
                g = term if g is None else g + term
            blk = pre[e * nk:(e + 1) * nk, lanes].astype(BF16)
            act = (0.5 * blk) * (1.0 + lax.erf(blk * (2.0 ** -0.5)))
            w_scr[pl.ds(e * nk, nk), lanes] = act * g
    acc_scr[...] += _dot(vt_ref[...], w_scr[...])

    @pl.when(j == pl.num_programs(1) - 1)
    def _():
        o_ref[...] = x1_ref[...] + acc_scr[...].T


def _peer_dense(h2t, u_tab, vt_tab, r2, p2, n1, p1, x1, tl, ec):
    d, n = h2t.shape
    ne = u_tab.shape[0]
    hspec = pl.BlockSpec((PEER_HEADS, PEER_N_KEYS, tl), lambda i, j: (0, 0, i))
    bspec = pl.BlockSpec((PEER_HEADS * PEER_N_KEYS, tl), lambda i, j: (0, i))
    return pl.pallas_call(
        functools.partial(_dense_kernel, ec=ec),
        out_shape=jax.ShapeDtypeStruct((n, d), F32),
        grid=(n // tl, ne // ec),
        in_specs=[
            pl.BlockSpec((d, tl), lambda i, j: (0, i)),
            pl.BlockSpec((ec, d), lambda i, j: (j, 0)),
            pl.BlockSpec((d, ec), lambda i, j: (0, j)),
            bspec, bspec, hspec, hspec,
            pl.BlockSpec((tl, d), lambda i, j: (i, 0)),
        ],
        out_specs=pl.BlockSpec((tl, d), lambda i, j: (i, 0)),
        scratch_shapes=[pltpu.VMEM((d, tl), F32), pltpu.VMEM((ec, tl), BF16),
                        pltpu.VMEM((PEER_HEADS * PEER_N_KEYS, tl), BF16),
                        pltpu.VMEM((PEER_HEADS * PEER_N_KEYS, tl), BF16)],
        compiler_params=pltpu.CompilerParams(
            dimension_semantics=("parallel", "arbitrary"), vmem_limit_bytes=VMEM_LIMIT),
        name="peer_dense",
    )(h2t, u_tab, vt_tab, r2, p2, n1, p1, x1)


def _tile(n, pref):
    t = min(n, pref)
    assert n % t == 0, (n, t)
    return t


def kernel(x, mix_norm_w, w_in, sb_q_norm_w, sb_k_norm_w, gdn_conv_w, gdn_a_log, gdn_dt_bias,
           gdn_out_norm_w, w_branch_sb, w_branch_gdn, w_out, ffn_norm_w, peer_w_q, peer_keys1,
           peer_keys2, peer_u, peer_v):
    b, s, d = x.shape
    n = b * s
    depth = w_in.shape[0]
    sb_w = HEADS * HEAD_DIM
    small0 = 7 * sb_w
    small1 = small0 + 2 * HEADS
    nchunk = s // GDN_CHUNK
    x2d = x.reshape(n, d)
    for l in range(depth):
        w_main = jnp.concatenate([w_in[l][:, :small0], w_in[l][:, small1:]], axis=1).astype(BF16)
        w_small = jnp.pad(w_in[l][:, small0:small1], ((0, 0), (0, LANES - 2 * HEADS))).astype(BF16)
        proj_main, proj_small = _in_proj(x2d, mix_norm_w[l][None], w_main, w_small,
                                         _tile(n, IN_PROJ_ROWS), IN_PROJ_COLS)
        proj3 = proj_main.reshape(b, s, MAIN_WIDTH)

        o_sb = _sb_attention(proj3, sb_q_norm_w[l][None], sb_k_norm_w[l][None],
                             _tile(s, SB_QUERY_ROWS))

        ba = proj_small[:, :2 * HEADS].reshape(b, nchunk, GDN_CHUNK, 2 * HEADS)
        ba = ba.transpose(0, 3, 1, 2)
        o_gdn = _gdn(proj3, gdn_conv_w[l], gdn_a_log[l], gdn_dt_bias[l], ba[:, HEADS:], ba[:, :HEADS],
                     gdn_out_norm_w[l][None])

        x1, h2t, s1t, s2t = _merge(
            o_sb.reshape(n, sb_w), o_gdn.reshape(n, sb_w), proj_main, x2d,
            w_branch_sb[l].astype(BF16), w_branch_gdn[l].astype(BF16), w_out[l].astype(BF16),
            ffn_norm_w[l][None], peer_w_q[l].astype(BF16),
            peer_keys1[l].astype(BF16), peer_keys2[l].astype(BF16), _tile(n, MERGE_ROWS))

        n1, p1, r2, p2 = _peer_topk(s1t, s2t, _tile(n, TOPK_TOKENS))
        x2d = _peer_dense(h2t, peer_u[l].astype(BF16), peer_v[l].astype(BF16).T,
                          r2, p2, n1, p1, x1, _tile(n, DENSE_TOKENS), DENSE_EXPERTS)
    return x2d.reshape(b, s, d)
```

```python
import functools

import jax
import jax.numpy as jnp
import numpy as np
from jax import lax
from jax.experimental import pallas as pl
from jax.experimental.pallas import tpu as pltpu

F32 = jnp.float32
BF16 = jnp.bfloat16

D_MODEL = 1024
HEADS = 8
HEAD_DIM = 128
LANES = 128
GDN_CONV = 4
GDN_CHUNK = 64
PEER_HEADS = 8
PEER_N_KEYS = 128
PEER_HALF = 128
PEER_TOPK = 16
EPS = 1e-6
MAIN_WIDTH = 9 * D_MODEL
VMEM_LIMIT = 56 * 1024 * 1024
SB_LOG_UNDERFLOW = -104.0
SB_GROUP = 8
GDN_GROUP = 2
GDN_UNROLL = 8
TOPK_UNROLL = 8
IN_PROJ_ROWS, IN_PROJ_COLS = 1024, 4608
SB_QUERY_ROWS = 256
MERGE_ROWS = 512
TOPK_TOKENS = 2048
DENSE_TOKENS, DENSE_EXPERTS = 512, 2048
NEG_INF = float("-inf")
POS_INF = float("inf")


def _sigmoid(x):
    return 1.0 / (1.0 + jnp.exp(-x))


def _softplus(x):
    return jnp.maximum(x, 0.0) + jnp.log1p(jnp.exp(-jnp.abs(x)))


def _rms(x, w):
    return x * lax.rsqrt(jnp.mean(x * x, axis=-1, keepdims=True) + EPS) * w


def _dot(a, b):
    return jnp.dot(a, b, preferred_element_type=F32)


def _dot_nt(a, b):
    return lax.dot_general(a, b, (((1,), (1,)), ((), ())), preferred_element_type=F32)


def _dot_tn(a, b):
    return lax.dot_general(a, b, (((0,), (0,)), ((), ())), preferred_element_type=F32)


def _inproj_kernel(x_ref, nw_ref, w_ref, ws_ref, o_ref, os_ref, h_scr):
    j = pl.program_id(1)

    @pl.when(j == 0)
    def _():
        h = _rms(x_ref[...], nw_ref[...]).astype(BF16)
        h_scr[...] = h
        os_ref[...] = _dot(h, ws_ref[...])

    o_ref[...] = _dot(h_scr[...], w_ref[...]).astype(o_ref.dtype)


def _in_proj(x2d, norm_w, w_main, w_small, tm, tn):
    n = x2d.shape[0]
    return pl.pallas_call(
        _inproj_kernel,
        out_shape=(jax.ShapeDtypeStruct((n, MAIN_WIDTH), BF16),
                   jax.ShapeDtypeStruct((n, LANES), F32)),
        grid=(n // tm, MAIN_WIDTH // tn),
        in_specs=[
            pl.BlockSpec((tm, D_MODEL), lambda i, j: (i, 0)),
            pl.BlockSpec((1, D_MODEL), lambda i, j: (0, 0)),
            pl.BlockSpec((D_MODEL, tn), lambda i, j: (0, j)),
            pl.BlockSpec((D_MODEL, LANES), lambda i, j: (0, 0)),
        ],
        out_specs=(pl.BlockSpec((tm, tn), lambda i, j: (i, j)),
                   pl.BlockSpec((tm, LANES), lambda i, j: (i, 0))),
        scratch_shapes=[pltpu.VMEM((tm, D_MODEL), BF16)],
        compiler_params=pltpu.CompilerParams(
            dimension_semantics=("parallel", "arbitrary"), vmem_limit_bytes=VMEM_LIMIT),
        name="in_proj",
    )(x2d, norm_w, w_main, w_small)


def _sb_kernel(q_ref, k_ref, v_ref, qw_ref, kw_ref, tri_ref, o_ref, kn_scr, *, tq):
    i = pl.program_id(2)
    kb = LANES
    dh = HEAD_DIM
    heads = range(SB_GROUP)

    @pl.when(i == 0)
    def _():
        for g in heads:
            kn_scr[:, g * dh:(g + 1) * dh] = _rms(
                k_ref[0, :, g * dh:(g + 1) * dh].astype(F32), kw_ref[...]).astype(BF16)

    scale = dh ** -0.5
    qn = [(_rms(q_ref[0, :, g * dh:(g + 1) * dh].astype(F32), qw_ref[...]) * scale).astype(BF16)
          for g in heads]
    row = i * tq + lax.broadcasted_iota(jnp.int32, (tq, kb), 0)
    col0 = lax.broadcasted_iota(jnp.int32, (tq, kb), 1)
    npair = (i + 1) * (tq // (2 * kb))

    def cond(carry):
        p, cs, _ = carry
        cmax = cs[0]
        for g in heads[1:]:
            cmax = jnp.maximum(cmax, cs[g])
        return jnp.logical_and(p < npair, jnp.max(cmax) > SB_LOG_UNDERFLOW)

    def body(carry):
        p, cs, accs = carry
        k0 = pl.multiple_of((npair - 1 - p) * (2 * kb), 2 * kb)
        masks = [(col0 + (k0 + half * kb)) < row for half in range(2)]
        z2 = [_dot_nt(qn[g], kn_scr[pl.ds(k0, 2 * kb), g * dh:(g + 1) * dh]) for g in heads]
        cats, logits = [], []
        for g in heads:
            cat_g, logit_g = [], []
            for half in range(2):
                z = z2[g][:, half * kb:(half + 1) * kb]
                sp = jnp.maximum(z, 0.0) + jnp.log(1.0 + jnp.exp(-jnp.abs(z)))
                lneg = jnp.where(masks[half], -sp, 0.0)
                hi = lneg.astype(BF16)
                lo = (lneg - hi.astype(F32)).astype(BF16)
                cat_g.append(jnp.concatenate([hi, lo], axis=1))
                logit_g.append(z - sp)
            cats.append(jnp.concatenate(cat_g, axis=0))
            logits.append(logit_g)
        res = [_dot(cats[g], tri_ref[...]) for g in heads]
        a2, c_new = [], []
        for g in heads:
            c_early = cs[g] + res[g][tq:, kb:]
            a_late = jnp.where(masks[1], jnp.exp(logits[g][1] + res[g][tq:, :kb] + cs[g]), 0.0)
            a_early = jnp.where(masks[0], jnp.exp(logits[g][0] + res[g][:tq, :kb] + c_early), 0.0)
            a2.append(jnp.concatenate([a_early, a_late], axis=1).astype(BF16))
            c_new.append(c_early + res[g][:tq, kb:])
        accs = tuple(accs[g] + _dot(a2[g], v_ref[0, pl.ds(k0, 2 * kb), g * dh:(g + 1) * dh])
                     for g in heads)
        return p + 1, tuple(c_new), accs

    init = (jnp.int32(0), tuple(jnp.zeros((tq, kb), F32) for _ in heads),
            tuple(jnp.zeros((tq, dh), F32) for _ in heads))
    _, _, accs = lax.while_loop(cond, body, init)
    for g in heads:
        o_ref[0, :, g * dh:(g + 1) * dh] = accs[g].astype(o_ref.dtype)


def _sb_tri():
    r = np.arange(2 * LANES)[:, None] % LANES
    c = np.arange(2 * LANES)[None, :]
    m = np.where(c < LANES, r > c, True)
    return jnp.asarray(m, dtype=BF16)


def _sb_attention(proj3, qw, kw, tq):
    b, s, _ = proj3.shape
    w = SB_GROUP * HEAD_DIM
    ng = HEADS // SB_GROUP
    return pl.pallas_call(
        functools.partial(_sb_kernel, tq=tq),
        out_shape=jax.ShapeDtypeStruct((b, s, HEADS * HEAD_DIM), BF16),
        grid=(b, ng, s // tq),
        in_specs=[
            pl.BlockSpec((1, tq, w), lambda bi, h, i: (bi, i, h)),
            pl.BlockSpec((1, s, w), lambda bi, h, i: (bi, 0, ng + h)),
            pl.BlockSpec((1, s, w), lambda bi, h, i: (bi, 0, 2 * ng + h)),
            pl.BlockSpec((1, HEAD_DIM), lambda bi, h, i: (0, 0)),
            pl.BlockSpec((1, HEAD_DIM), lambda bi, h, i: (0, 0)),
            pl.BlockSpec((2 * LANES, 2 * LANES), lambda bi, h, i: (0, 0)),
        ],
        out_specs=pl.BlockSpec((1, tq, w), lambda bi, h, i: (bi, i, h)),
        scratch_shapes=[pltpu.VMEM((s, w), BF16)],
        compiler_params=pltpu.CompilerParams(
            dimension_semantics=("parallel", "parallel", "arbitrary"),
            vmem_limit_bytes=VMEM_LIMIT),
        name="sb_attn",
    )(proj3, proj3, proj3, qw, kw, _sb_tri())


def _gdn_kernel(alog_ref, dtb_ref, gq_ref, gk_ref, gv_ref, z_ref, cwq_ref, cwk_ref, cwv_ref,
                a_ref, b_ref, onw_ref, o_ref,
                xpad, qs, ks, vs, m_scr, b_scr, qp_scr, op_scr, eg_scr):
    hg = pl.program_id(1)
    s = qs.shape[0]
    c = GDN_CHUNK
    dk = HEAD_DIM
    w = GDN_GROUP * dk
    pad = 8

    def conv_silu(src_ref, cw_ref):
        xpad[pl.ds(0, pad), :] = jnp.zeros((pad, w), F32)
        xpad[pl.ds(pad, s), :] = src_ref[0].astype(F32)
        y = cw_ref[0:1, :] * xpad[pl.ds(pad - 3, s), :]
        for t in range(1, GDN_CONV):
            y = y + cw_ref[t:t + 1, :] * xpad[pl.ds(pad - 3 + t, s), :]
        return y * _sigmoid(y)

    def l2n(x):
        return x * lax.rsqrt(jnp.sum(x * x, axis=-1, keepdims=True) + EPS)

    yq = conv_silu(gq_ref, cwq_ref)
    for g in range(GDN_GROUP):
        qs[:, g * dk:(g + 1) * dk] = l2n(yq[:, g * dk:(g + 1) * dk]) * (dk ** -0.5)
    yk = conv_silu(gk_ref, cwk_ref)
    for g in range(GDN_GROUP):
        ks[:, g * dk:(g + 1) * dk] = l2n(yk[:, g * dk:(g + 1) * dk])
    vs[...] = conv_silu(gv_ref, cwv_ref)

    a_gain = [jnp.exp(jnp.full((1, 1), alog_ref[hg * GDN_GROUP + g], F32)) for g in range(GDN_GROUP)]
    dtb = [jnp.full((1, 1), dtb_ref[hg * GDN_GROUP + g], F32) for g in range(GDN_GROUP)]

    ri = lax.broadcasted_iota(jnp.int32, (c, c), 0)
    ci = lax.broadcasted_iota(jnp.int32, (c, c), 1)
    tril = ri >= ci
    strict = ri > ci
    ones_b = jnp.ones((2 * c, dk), BF16)
    triu_b = jnp.where(ri <= ci, 1.0, 0.0).astype(BF16)
    eye = jnp.where(ri == ci, 1.0, 0.0)

    def split(x):
        hi = x.astype(BF16)
        return hi, (x - hi.astype(F32)).astype(BF16)

    def prepare(nn):
        pairs = [(g, nn * GDN_UNROLL + j) for j in range(GDN_UNROLL) for g in range(GDN_GROUP)]
        ps = range(len(pairs))
        rows = [pl.ds(pl.multiple_of(n * c, c), c) for _, n in pairs]
        q = [qs[rows[i], g * dk:(g + 1) * dk] for i, (g, _) in enumerate(pairs)]
        k = [ks[rows[i], g * dk:(g + 1) * dk] for i, (g, _) in enumerate(pairs)]
        v = [vs[rows[i], g * dk:(g + 1) * dk] for i, (g, _) in enumerate(pairs)]
        g_row = [-a_gain[g] * _softplus(a_ref[0, g, pl.ds(n, 1), :] + dtb[g]) for g, n in pairs]
        b_row = [_sigmoid(b_ref[0, g, pl.ds(n, 1), :]) for g, n in pairs]
        col_src = [jnp.concatenate([jnp.where(tril, jnp.broadcast_to(g_row[i], (c, c)), 0.0),
                                    jnp.where(ri == ci, jnp.broadcast_to(b_row[i], (c, c)), 0.0)],
                                   axis=0) for i in ps]
        col2 = [_dot(jnp.concatenate(split(col_src[i]), axis=1), ones_b) for i in ps]
        gc = [col2[i][:c] for i in ps]
        beta = [col2[i][c:] for i in ps]
        grs = [jnp.concatenate(split(jnp.broadcast_to(g_row[i], (8, c))), axis=0) for i in ps]
        gr2 = [_dot(grs[i], triu_b) for i in ps]
        gc_row = [gr2[i][0:1] + gr2[i][8:9] for i in ps]
        yield
        decay = [jnp.where(tril, jnp.exp(jnp.where(tril, gc[i][:, :c] - gc_row[i], 0.0)), 0.0)
                 for i in ps]
        kbf = [k[i].astype(BF16) for i in ps]
        k_beta = [k[i] * beta[i] for i in ps]
        u = [-jnp.where(strict, _dot_nt(k_beta[i].astype(BF16), kbf[i]) * decay[i], 0.0) for i in ps]
        mb = [u[i].astype(BF16) for i in ps]
        m2 = [_dot(mb[i], mb[i]) for i in ps]
        yield
        for level in range(5):
            mb = [m2[i].astype(BF16) for i in ps]
            if level < 4:
                both = [_dot(jnp.concatenate([mb[i], u[i].astype(BF16)], axis=0), mb[i]) for i in ps]
                u = [u[i] + m2[i] + both[i][c:] for i in ps]
                m2 = [both[i][:c] for i in ps]
            else:
                u = [u[i] + m2[i] + _dot(u[i].astype(BF16), mb[i]) for i in ps]
            yield
        tb = [(eye + u[i]).astype(BF16) for i in ps]
        egc = [jnp.exp(gc[i]) for i in ps]
        kv_b = [_dot(tb[i], jnp.concatenate([(k_beta[i] * egc[i]).astype(BF16),
                                             (v[i] * beta[i]).astype(BF16)], axis=1)).astype(BF16)
                for i in ps]
        attn_b = [jnp.where(tril, _dot_nt(q[i].astype(BF16), kbf[i]) * decay[i], 0.0).astype(BF16)
                  for i in ps]
        yield
        g_last = [gc[i][c - 1:c, :] for i in ps]
        kw_b = [(k[i] * jnp.exp(g_last[i] - gc[i])).astype(BF16) for i in ps]
        mb_out = [_dot_tn(kw_b[i], kv_b[i]) for i in ps]
        qo_out = [_dot(attn_b[i], kv_b[i]) for i in ps]
        m_out = [mb_out[i][:, :dk].astype(BF16) for i in ps]
        b_out = [mb_out[i][:, dk:] for i in ps]
        qp_out = [(q[i] * egc[i] - qo_out[i][:, :dk]).astype(BF16) for i in ps]
        op_out = [qo_out[i][:, dk:] for i in ps]
        for i, (g, n) in enumerate(pairs):
            m_scr[g, n] = m_out[i]
            b_scr[g, n] = b_out[i]
            qp_scr[g, n] = qp_out[i]
            op_scr[g, n] = op_out[i]
            eg_scr[g, n] = jnp.broadcast_to(jnp.exp(g_last[i]), (8, dk))

    def scan(n, states):
        r0 = pl.multiple_of(n * c, c)
        gs = range(GDN_GROUP)
        sb = [states[g].astype(BF16) for g in gs]
        zc = [z_ref[0, pl.ds(r0, c), g * dk:(g + 1) * dk].astype(F32) for g in gs]
        o = [_dot(qp_scr[g, n], sb[g]) + op_scr[g, n] for g in gs]
        new = tuple(states[g] * eg_scr[g, n][0:1] - _dot(m_scr[g, n], sb[g]) + b_scr[g, n]
                    for g in gs)
        for g in gs:
            o_ref[0, pl.ds(r0, c), g * dk:(g + 1) * dk] = (
                _rms(o[g], onw_ref[...]) * (zc[g] * _sigmoid(zc[g]))).astype(o_ref.dtype)
        return new

    ngroup = s // (c * GDN_UNROLL)
    for _ in prepare(0):
        pass

    def body(nn, states):
        stages = prepare(nn)
        for step in range(GDN_UNROLL):
            next(stages, None)
            states = scan((nn - 1) * GDN_UNROLL + step, states)
        for _ in stages:
            pass
        return states

    states = lax.fori_loop(1, ngroup, body,
                           tuple(jnp.zeros((dk, dk), F32) for _ in range(GDN_GROUP)))
    lax.fori_loop((ngroup - 1) * GDN_UNROLL, ngroup * GDN_UNROLL, scan, states)


def _gdn(proj3, conv_w, a_log, dt_bias, a_rows, b_rows, out_norm_w):
    b, s, _ = proj3.shape
    n = s // GDN_CHUNK
    c = GDN_CHUNK
    gg = GDN_GROUP
    w = gg * HEAD_DIM
    assert s % (c * GDN_UNROLL) == 0, s
    col = lambda base: pl.BlockSpec((1, s, w), lambda bi, h: (bi, 0, base // gg + h))
    cw = lambda base: pl.BlockSpec((GDN_CONV, w), lambda bi, h: (0, base // gg + h))
    smem = pl.BlockSpec(memory_space=pltpu.SMEM)
    return pl.pallas_call(
        _gdn_kernel,
        out_shape=jax.ShapeDtypeStruct((b, s, HEADS * HEAD_DIM), BF16),
        grid=(b, HEADS // gg),
        in_specs=[
            smem, smem,
            col(3 * HEADS), col(4 * HEADS), col(5 * HEADS), col(6 * HEADS),
            cw(0), cw(HEADS), cw(2 * HEADS),
            pl.BlockSpec((1, gg, n, c), lambda bi, h: (bi, h, 0, 0)),
            pl.BlockSpec((1, gg, n, c), lambda bi, h: (bi, h, 0, 0)),
            pl.BlockSpec((1, HEAD_DIM), lambda bi, h: (0, 0)),
        ],
        out_specs=pl.BlockSpec((1, s, w), lambda bi, h: (bi, 0, h)),
        scratch_shapes=[
            pltpu.VMEM((s + 8, w), F32), pltpu.VMEM((s, w), F32), pltpu.VMEM((s, w), F32),
            pltpu.VMEM((s, w), F32),
            pltpu.VMEM((gg, n, HEAD_DIM, HEAD_DIM), BF16), pltpu.VMEM((gg, n, HEAD_DIM, HEAD_DIM), F32),
            pltpu.VMEM((gg, n, c, HEAD_DIM), BF16), pltpu.VMEM((gg, n, c, HEAD_DIM), F32),
            pltpu.VMEM((gg, n, 8, HEAD_DIM), F32),
        ],
        compiler_params=pltpu.CompilerParams(
            dimension_semantics=("parallel", "parallel"), vmem_limit_bytes=VMEM_LIMIT),
        name="gdn",
    )(a_log, dt_bias, proj3, proj3, proj3, proj3, conv_w, conv_w, conv_w,
      a_rows, b_rows, out_norm_w)


def _merge_kernel(osb_ref, ogdn_ref, gsb_ref, ggdn_ref, x_ref, wsb_ref, wgdn_ref, wout_ref,
                  fnw_ref, wq_ref, k1_ref, k2_ref, x1_ref, h2t_ref, s1_ref, s2_ref):
    y_sb = _dot(osb_ref[...], wsb_ref[...])
    y_gdn = _dot(ogdn_ref[...], wgdn_ref[...])
    merged = (_sigmoid(gsb_ref[...].astype(F32)) * y_sb
              + _sigmoid(ggdn_ref[...].astype(F32)) * y_gdn)
    x1 = x_ref[...] + _dot(merged.astype(BF16), wout_ref[...])
    x1_ref[...] = x1
    h2f = _rms(x1, fnw_ref[...])
    h2 = h2f.astype(BF16)
    h2t_ref[...] = h2f.T.astype(BF16)
    q = _dot(h2, wq_ref[...]).astype(BF16)
    for hh in range(PEER_HEADS):
        base = hh * 2 * PEER_HALF
        s1_ref[hh] = _dot_nt(k1_ref[hh], q[:, base:base + PEER_HALF])
        s2_ref[hh] = _dot_nt(k2_ref[hh], q[:, base + PEER_HALF:base + 2 * PEER_HALF])


def _merge(o_sb, o_gdn, proj_main, x2d, w_sb, w_gdn, w_out, ffn_w, w_q, keys1, keys2, tm):
    n = x2d.shape[0]
    full = lambda shape: pl.BlockSpec(shape, lambda i: (0,) * len(shape))
    tok = lambda width, cb=0: pl.BlockSpec((tm, width), lambda i, cb=cb: (i, cb))
    st_spec = pl.BlockSpec((PEER_HEADS, PEER_N_KEYS, tm), lambda i: (0, 0, i))
    return pl.pallas_call(
        _merge_kernel,
        out_shape=(jax.ShapeDtypeStruct((n, D_MODEL), F32),
                   jax.ShapeDtypeStruct((D_MODEL, n), BF16),
                   jax.ShapeDtypeStruct((PEER_HEADS, PEER_N_KEYS, n), F32),
                   jax.ShapeDtypeStruct((PEER_HEADS, PEER_N_KEYS, n), F32)),
        grid=(n // tm,),
        in_specs=[
            tok(D_MODEL), tok(D_MODEL), tok(D_MODEL, 7), tok(D_MODEL, 8), tok(D_MODEL),
            full((D_MODEL, D_MODEL)), full((D_MODEL, D_MODEL)), full((D_MODEL, D_MODEL)),
            full((1, D_MODEL)), full((D_MODEL, 2 * PEER_HALF * PEER_HEADS)),
            full((PEER_HEADS, PEER_N_KEYS, PEER_HALF)), full((PEER_HEADS, PEER_N_KEYS, PEER_HALF)),
        ],
        out_specs=(tok(D_MODEL), pl.BlockSpec((D_MODEL, tm), lambda i: (0, i)), st_spec, st_spec),
        compiler_params=pltpu.CompilerParams(
            dimension_semantics=("parallel",), vmem_limit_bytes=VMEM_LIMIT),
        name="merge",
    )(o_sb, o_gdn, proj_main, proj_main, x2d, w_sb, w_gdn, w_out, ffn_w, w_q, keys1, keys2)


def _top_ranked(arrays, k, companion=None):
    assert k <= 16
    big = 2.0 ** 100
    work = [jnp.maximum(s, -0.5 * big) for s in arrays]
    vals = [[] for _ in arrays]
    for r in range(k):
        ms = [jnp.max(s, axis=0, keepdims=True) for s in work]
        for a, m in enumerate(ms):
            vals[a].append(m)
        work = [jnp.where(s == m, -big * (1.0 + r / 16.0), s) for s, m in zip(work, ms)]
        if companion is not None:
            next(companion, None)
    ranks = [jnp.where(s <= -big, (s * (-1.0 / big) - 1.0) * 16.0, float(k)) for s in work]
    return [(jnp.concatenate(v, axis=0), rk) for v, rk in zip(vals, ranks)]


def _topk_kernel(s1_ref, s2_ref, n1_ref, p1_ref, r2_ref, p2_ref, *, tl):
    kk = PEER_TOPK

    def select(l0, s1, s2, v1, rank1, v2, rank2):
        jrow8 = lax.broadcasted_iota(jnp.int32, (8, LANES), 0)
        blocks = [v1[0:1] + v2]
        for i in range(1, 8):
            blocks.append(v1[i:i + 1] + jnp.where(jrow8 < kk // (i + 1), v2[0:8], NEG_INF))
        blocks.append(v1[8:16] + v2[0:1])
        cand = jnp.concatenate(blocks, axis=0)
        top = cand[0:1]
        work = cand
        tau = top
        for _ in range(kk):
            tau = jnp.max(work, axis=0, keepdims=True)
            work = jnp.where(work == tau, NEG_INF, work)
            yield
        sel = cand >= tau
        zsum = jnp.sum(jnp.where(sel, jnp.exp(jnp.where(sel, cand - top, 0.0)), 0.0),
                       axis=0, keepdims=True)
        counts = [jnp.sum(jnp.where(blocks[i] >= tau, 1.0, 0.0), axis=0, keepdims=True)
                  for i in range(8)]
        counts.append(jnp.where(blocks[8] >= tau, 1.0, 0.0))
        nrank = jnp.concatenate(counts, axis=0)
        n1 = jnp.zeros(s1.shape, F32)
        for i in range(kk):
            n1 = jnp.where(rank1 == float(i), nrank[i:i + 1], n1)
        n1_ref[0, :, pl.ds(l0, LANES)] = n1
        p1_ref[0, :, pl.ds(l0, LANES)] = jnp.exp(s1 - v1[0:1]) * (0.5 / zsum)
        r2_ref[:, pl.ds(l0, LANES)] = rank2.astype(BF16)
        p2_ref[:, pl.ds(l0, LANES)] = jnp.exp(s2 - v2[0:1]).astype(BF16)

    def sub(bi, _):
        pending = None
        for part in range(TOPK_UNROLL):
            l0 = pl.multiple_of((TOPK_UNROLL * bi + part) * LANES, LANES)
            s1 = s1_ref[0, :, pl.ds(l0, LANES)]
            s2 = s2_ref[0, :, pl.ds(l0, LANES)]
            (v1, rank1), (v2, rank2) = _top_ranked([s1, s2], kk, pending)
            if pending is not None:
                for _ in pending:
                    pass
            pending = select(l0, s1, s2, v1, rank1, v2, rank2)
        for _ in pending:
            pass
        return 0

    lax.fori_loop(0, tl // (TOPK_UNROLL * LANES), sub, 0)


def _peer_topk(s1t, s2t, tl):
    hh, kk, n = s1t.shape
    spec = pl.BlockSpec((1, kk, tl), lambda i, h: (h, 0, i))
    f32 = jax.ShapeDtypeStruct((hh, kk, n), F32)
    b16 = jax.ShapeDtypeStruct((hh * kk, n), BF16)
    spec2 = pl.BlockSpec((kk, tl), lambda i, h: (h, i))
    return pl.pallas_call(
        functools.partial(_topk_kernel, tl=tl),
        out_shape=(f32, f32, b16, b16),
        grid=(n // tl, hh),
        in_specs=[spec, spec],
        out_specs=(spec, spec, spec2, spec2),
        compiler_params=pltpu.CompilerParams(
            dimension_semantics=("parallel", "parallel"), vmem_limit_bytes=VMEM_LIMIT),
        name="peer_topk",
    )(s1t, s2t)


def _dense_kernel(h2t_ref, u_ref, vt_ref, r2_ref, p2_ref, n1_ref, p1_ref, x1_ref, o_ref,
                  acc_scr, w_scr, r2_scr, p2_scr, *, ec):
    j = pl.program_id(1)
    nk = PEER_N_KEYS
    tl = h2t_ref.shape[1]
    pk = 16

    @pl.when(j == 0)
    def _():
        acc_scr[...] = jnp.zeros_like(acc_scr)
        r2_scr[...] = r2_ref[...]
        p2_scr[...] = p2_ref[...]

    pre = _dot(u_ref[...], h2t_ref[...])
    for e in range(ec // nk):
        e1 = j * (ec // nk) + e
        n1_rows = [jnp.broadcast_to(n1_ref[hh, pl.ds(e1, 1), :], (pk, tl)).astype(BF16)
                   for hh in range(PEER_HEADS)]
        p1_rows = [jnp.broadcast_to(p1_ref[hh, pl.ds(e1, 1), :], (pk, tl)).astype(BF16)
                   for hh in range(PEER_HEADS)]
        for tv in range(tl // LANES):
            lanes = slice(tv * LANES, (tv + 1) * LANES)
            g = None
            for hh in range(PEER_HEADS):
                keys = slice(hh * nk, (hh + 1) * nk)
                n1 = jnp.concatenate([n1_rows[hh][:, lanes]] * (nk // pk), axis=0)
                p1 = jnp.concatenate([p1_rows[hh][:, lanes]] * (nk // pk), axis=0)
                p2 = p2_scr[keys, lanes]
                term = jnp.where(r2_scr[keys, lanes] < n1, p2, jnp.zeros_like(p2)) * p1
                g = term if g is None else g + term
            blk = pre[e * nk:(e + 1) * nk, lanes].astype(BF16)
            act = blk * (1.0 + lax.erf(blk * (2.0 ** -0.5)))
            w_scr[pl.ds(e * nk, nk), lanes] = act * g
    acc_scr[...] += _dot(vt_ref[...], w_scr[...])

    @pl.when(j == pl.num_programs(1) - 1)
    def _():
        o_ref[...] = x1_ref[...] + acc_scr[...].T


def _peer_dense(h2t, u_tab, vt_tab, r2, p2, n1, p1, x1, tl, ec):
    d, n = h2t.shape
    ne = u_tab.shape[0]
    hspec = pl.BlockSpec((PEER_HEADS, PEER_N_KEYS, tl), lambda i, j: (0, 0, i))
    bspec = pl.BlockSpec((PEER_HEADS * PEER_N_KEYS, tl), lambda i, j: (0, i))
    return pl.pallas_call(
        functools.partial(_dense_kernel, ec=ec),
        out_shape=jax.ShapeDtypeStruct((n, d), F32),
        grid=(n // tl, ne // ec),
        in_specs=[
            pl.BlockSpec((d, tl), lambda i, j: (0, i)),
            pl.BlockSpec((ec, d), lambda i, j: (j, 0)),
            pl.BlockSpec((d, ec), lambda i, j: (0, j)),
            bspec, bspec, hspec, hspec,
            pl.BlockSpec((tl, d), lambda i, j: (i, 0)),
        ],
        out_specs=pl.BlockSpec((tl, d), lambda i, j: (i, 0)),
        scratch_shapes=[pltpu.VMEM((d, tl), F32), pltpu.VMEM((ec, tl), BF16),
                        pltpu.VMEM((PEER_HEADS * PEER_N_KEYS, tl), BF16),
                        pltpu.VMEM((PEER_HEADS * PEER_N_KEYS, tl), BF16)],
        compiler_params=pltpu.CompilerParams(
            dimension_semantics=("parallel", "arbitrary"), vmem_limit_bytes=VMEM_LIMIT),
        name="peer_dense",
    )(h2t, u_tab, vt_tab, r2, p2, n1, p1, x1)


def _tile(n, pref):
    t = min(n, pref)
    assert n % t == 0, (n, t)
    return t


def kernel(x, mix_norm_w, w_in, sb_q_norm_w, sb_k_norm_w, gdn_conv_w, gdn_a_log, gdn_dt_bias,
           gdn_out_norm_w, w_branch_sb, w_branch_gdn, w_out, ffn_norm_w, peer_w_q, peer_keys1,
           peer_keys2, peer_u, peer_v):
    b, s, d = x.shape
    n = b * s
    depth = w_in.shape[0]
    sb_w = HEADS * HEAD_DIM
    small0 = 7 * sb_w
    small1 = small0 + 2 * HEADS
    nchunk = s // GDN_CHUNK
    x2d = x.reshape(n, d)
    for l in range(depth):
        w_main = jnp.concatenate([w_in[l][:, :small0], w_in[l][:, small1:]], axis=1).astype(BF16)
        w_small = jnp.pad(w_in[l][:, small0:small1], ((0, 0), (0, LANES - 2 * HEADS))).astype(BF16)
        proj_main, proj_small = _in_proj(x2d, mix_norm_w[l][None], w_main, w_small,
                                         _tile(n, IN_PROJ_ROWS), IN_PROJ_COLS)
        proj3 = proj_main.reshape(b, s, MAIN_WIDTH)

        o_sb = _sb_attention(proj3, sb_q_norm_w[l][None], sb_k_norm_w[l][None],
                             _tile(s, SB_QUERY_ROWS))

        ba = proj_small[:, :2 * HEADS].reshape(b, nchunk, GDN_CHUNK, 2 * HEADS)
        ba = ba.transpose(0, 3, 1, 2)
        o_gdn = _gdn(proj3, gdn_conv_w[l], gdn_a_log[l], gdn_dt_bias[l], ba[:, HEADS:], ba[:, :HEADS],
                     gdn_out_norm_w[l][None])

        x1, h2t, s1t, s2t = _merge(
            o_sb.reshape(n, sb_w), o_gdn.reshape(n, sb_w), proj_main, x2d,
            w_branch_sb[l].astype(BF16), w_branch_gdn[l].astype(BF16), w_out[l].astype(BF16),
            ffn_norm_w[l][None], peer_w_q[l].astype(BF16),
            peer_keys1[l].astype(BF16), peer_keys2[l].astype(BF16), _tile(n, MERGE_ROWS))

        n1, p1, r2, p2 = _peer_topk(s1t, s2t, _tile(n, TOPK_TOKENS))
        x2d = _peer_dense(h2t, peer_u[l].astype(BF16), peer_v[l].astype(BF16).T,
                          r2, p2, n1, p1, x1, _tile(n, DENSE_TOKENS), DENSE_EXPERTS)
    return x2d.reshape(b, s, d)
```
